```python
import jax, jax.numpy as jnp
from jax import lax
import numpy as np

D_MODEL = 1024
BATCH = 8
SEQ = 2048
DEPTH = 4
DEC_BATCH = 128
DEC_SEQ = 4
PAST_LEN = 2048
PAGE_SIZE = 128

N_A_LAYERS = DEPTH // 2
N_B_LAYERS = DEPTH - N_A_LAYERS
D_FF = -(-8 * D_MODEL // (3 * 256)) * 256
CHUNK = 128
D_A = 2 * D_MODEL
A_GROUPS = 8
HEAD_DIM = 64
N_HEADS = D_MODEL // HEAD_DIM
N_KV_HEADS = 4
Q_PER_KV = N_HEADS // N_KV_HEADS
ROPE_DIM = HEAD_DIM // 4
ROPE_THETA = 500000.0
CMP_STRIDE = 16
CMP_LEN = 2 * CMP_STRIDE
SEL_BLOCK = 64
SEL_TOPK = 16
WINDOW = 512
Q_BLOCK = 128
N_BRANCH = 3
EPS = 1e-6
ATTN_SCALE = HEAD_DIM ** -0.5

kernel_name = 'yoco_gmlp_nsa_decode_step'


def rmsnorm(x, g):
    xf = x.astype(jnp.float32)
    y = xf * lax.rsqrt(jnp.mean(xf * xf, axis=-1, keepdims=True) + EPS)
    return (y * g.astype(jnp.float32)).astype(x.dtype)


def layernorm(x, g, b):
    xf = x.astype(jnp.float32)
    xc = xf - jnp.mean(xf, axis=-1, keepdims=True)
    y = xc * lax.rsqrt(jnp.mean(xc * xc, axis=-1, keepdims=True) + EPS)
    return (y * g.astype(jnp.float32) + b.astype(jnp.float32)).astype(x.dtype)


def rope(x, pos):
    half = ROPE_DIM // 2
    inv_freq = ROPE_THETA ** (-jnp.arange(half, dtype=jnp.float32) / half)
    ang = pos.astype(jnp.float32)[:, None] * inv_freq[None, :]
    ang = ang.reshape((1, pos.shape[0]) + (1,) * (x.ndim - 3) + (half,))
    cos, sin = jnp.cos(ang), jnp.sin(ang)
    xf = x.astype(jnp.float32)
    x1, x2 = xf[..., :half], xf[..., half:ROPE_DIM]
    out = jnp.concatenate([x1 * cos - x2 * sin, x2 * cos + x1 * sin, xf[..., ROPE_DIM:]], axis=-1)
    return out.astype(x.dtype)


def masked_softmax(s, mask):
    s = jnp.where(mask, s, -jnp.inf)
    m = jnp.max(s, axis=-1, keepdims=True)
    m = jnp.where(jnp.isfinite(m), m, 0.0)
    e = jnp.where(mask, jnp.exp(s - m), 0.0)
    return e / jnp.maximum(jnp.sum(e, axis=-1, keepdims=True), 1e-30)


def swiglu(x, w_gate, w_up, w_down):
    return (jax.nn.silu(x @ w_gate) * (x @ w_up)) @ w_down


def pad_rows(x, mult):
    t = x.shape[1]
    tp = -(-t // mult) * mult
    return jnp.pad(x, ((0, 0), (0, tp - t)) + ((0, 0),) * (x.ndim - 2))


def chunk_gmlp(x, w_in, ln_g, ln_b, w_s, b_s, w_out):
    B, T, _ = x.shape
    uv = jax.nn.gelu(x @ w_in)
    u, v = uv[..., :D_A], uv[..., D_A:]
    v = layernorm(v, ln_g, ln_b)
    lc = CHUNK if T >= CHUNK else T
    nc = T // lc
    causal = jnp.tril(jnp.ones((lc, lc), dtype=bool))
    ws = jnp.where(causal[None], w_s[:, :lc, :lc], 0.0)
    vc = v.reshape(B, nc, lc, A_GROUPS, D_A // A_GROUPS)
    sv = jnp.einsum('gts,bcsgd->bctgd', ws, vc) + b_s[:, :lc].T[None, None, :, :, None]
    y = (u * sv.reshape(B, T, D_A)) @ w_out
    return y, v


def kv_project(h, pos, kv_norm, kv_w, k_norm_slc, k_norm_win):
    B, T, _ = h.shape
    kv = (rmsnorm(h, kv_norm) @ kv_w).reshape(B, T, N_BRANCH, 2, N_KV_HEADS, HEAD_DIM)
    kv_cmp = kv[:, :, 0]
    k_slc = rope(rmsnorm(kv[:, :, 1, 0], k_norm_slc), pos)
    k_win = rope(rmsnorm(kv[:, :, 2, 0], k_norm_win), pos)
    kv_slc = jnp.stack([k_slc, kv[:, :, 1, 1]], axis=2)
    kv_win = jnp.stack([k_win, kv[:, :, 2, 1]], axis=2)
    return kv_cmp, kv_slc, kv_win


def compress(rows, pos_k, pos_v, w_k, w_v, k_norm):
    B, T = rows.shape[:2]
    sub = rows.reshape((B, T // CMP_STRIDE, CMP_STRIDE) + rows.shape[2:])
    blk = jnp.concatenate([sub[:, :-1], sub[:, 1:]], axis=2)
    kc = jnp.einsum('bnlgd,lde->bnge', blk[:, :, :, 0] + pos_k[:, None, :], w_k)
    vc = jnp.einsum('bnlgd,lde->bnge', blk[:, :, :, 1] + pos_v[:, None, :], w_v)
    return rmsnorm(kc, k_norm), vc


def cmp_branch(q_n, q_pos, kc, vc):
    n_cmp = kc.shape[1]
    blk_end = jnp.arange(n_cmp) * CMP_STRIDE + CMP_LEN - 1
    mask = blk_end[None, :] <= q_pos[:, None]
    s = jnp.einsum('btgrd,bngd->btgrn', q_n, kc).astype(jnp.float32) * ATTN_SCALE
    p = masked_softmax(s, mask[None, :, None, None, :])
    o = jnp.einsum('btgrn,bngd->btgrd', p.astype(vc.dtype), vc)
    return o, jnp.sum(p, axis=3)


def sel_importance(imp_cmp, n_sel):
    n_cmp = imp_cmp.shape[-1]
    i = jnp.arange(n_cmp)[:, None] * CMP_STRIDE
    j = jnp.arange(n_sel)[None, :] * SEL_BLOCK
    m = ((i < j + SEL_BLOCK) & (i + CMP_LEN > j)).astype(jnp.float32)
    return imp_cmp @ m


def slc_one(q_rot, q_pos, imp_slc, kv_blocks):
    n_sel = imp_slc.shape[-1]
    blk = jnp.arange(n_sel)[None, :]
    cur = (q_pos // SEL_BLOCK)[:, None]
    valid = blk <= cur
    forced = (blk == 0) | (blk == cur) | (blk == cur - 1)
    score = jnp.where(forced[:, None], jnp.inf, jnp.where(valid[:, None], imp_slc, -jnp.inf))
    top_s, idx = lax.top_k(score, min(SEL_TOPK, n_sel))
    g_idx = jnp.arange(N_KV_HEADS)[None, :, None]
    k_g = kv_blocks[0][g_idx, idx]
    v_g = kv_blocks[1][g_idx, idx]
    key_pos = idx[..., None] * SEL_BLOCK + jnp.arange(SEL_BLOCK)
    mask = (top_s > -jnp.inf)[..., None] & (key_pos <= q_pos[:, None, None, None])
    s = jnp.einsum('tgrd,tgksd->tgrks', q_rot, k_g).astype(jnp.float32) * ATTN_SCALE
    tq, g, r, k, sb = s.shape
    p = masked_softmax(s.reshape(tq, g, r, k * sb), mask.reshape(tq, g, 1, k * sb))
    return jnp.einsum('tgrks,tgksd->tgrd', p.reshape(s.shape).astype(v_g.dtype), v_g)


def slc_branch(q_rot, q_pos, imp_slc, kv_rows):
    B, T = q_rot.shape[:2]
    qb = min(Q_BLOCK, T)
    n_qb = T // qb
    n_sel = kv_rows.shape[1] // SEL_BLOCK
    kv_blocks = kv_rows.reshape(B, n_sel, SEL_BLOCK, 2, N_KV_HEADS, HEAD_DIM).transpose(0, 3, 4, 1, 2, 5)
    xs = (q_rot.reshape(B * n_qb, qb, N_KV_HEADS, Q_PER_KV, HEAD_DIM),
          jnp.tile(q_pos.reshape(n_qb, qb), (B, 1)),
          imp_slc.reshape(B * n_qb, qb, N_KV_HEADS, n_sel),
          jnp.repeat(jnp.arange(B), n_qb))
    o = lax.map(lambda a: slc_one(a[0], a[1], a[2], kv_blocks[a[3]]), xs)
    return o.reshape(B, T, N_KV_HEADS, Q_PER_KV, HEAD_DIM)


def win_attend(q_rot, q_pos, kv, k_pos):
    kp, qp = k_pos[None, :], q_pos[:, None]
    mask = (kp <= qp) & (kp > qp - WINDOW) & (kp >= 0)
    s = jnp.einsum('btgrd,bsgd->btgrs', q_rot, kv[:, :, 0]).astype(jnp.float32) * ATTN_SCALE
    p = masked_softmax(s, mask[None, :, None, None, :])
    return jnp.einsum('btgrs,bsgd->btgrd', p.astype(kv.dtype), kv[:, :, 1])


def window_prompt(q_rot, kv_win):
    B, T = q_rot.shape[:2]
    qb = min(Q_BLOCK, T)
    n_qb = T // qb
    kv_pad = jnp.pad(kv_win, ((0, 0), (WINDOW, 0), (0, 0), (0, 0), (0, 0)))
    q_blocks = jnp.moveaxis(q_rot.reshape(B, n_qb, qb, N_KV_HEADS, Q_PER_KV, HEAD_DIM), 1, 0)

    def body(args):
        q_blk, n = args
        start = n * qb
        kv_blk = lax.dynamic_slice_in_dim(kv_pad, start, WINDOW + qb, axis=1)
        k_pos = start - WINDOW + jnp.arange(WINDOW + qb)
        return win_attend(q_blk, start + jnp.arange(qb), kv_blk, k_pos)

    o = lax.map(body, (q_blocks, jnp.arange(n_qb)))
    return jnp.moveaxis(o, 0, 1).reshape(B, T, N_KV_HEADS, Q_PER_KV, HEAD_DIM)


def nsa_mixer(xn, pos, w_in, q_norm, w_out, branches):
    B, T, _ = xn.shape
    proj = xn @ w_in
    q = proj[..., :N_HEADS * HEAD_DIM].reshape(B, T, N_KV_HEADS, Q_PER_KV, HEAD_DIM)
    gates = jax.nn.sigmoid(proj[..., N_HEADS * HEAD_DIM:].astype(jnp.float32)).reshape(B, T, N_KV_HEADS, Q_PER_KV, N_BRANCH)
    q_n = rmsnorm(q, q_norm)
    o_cmp, o_slc, o_win = branches(q_n, rope(q_n, pos))
    o = (gates[..., 0:1] * o_cmp.astype(jnp.float32) + gates[..., 1:2] * o_slc.astype(jnp.float32)
         + gates[..., 2:3] * o_win.astype(jnp.float32))
    return o.astype(xn.dtype).reshape(B, T, N_HEADS * HEAD_DIM) @ w_out


def trunk(x, pos, make_branches, norm_mix, norm_ffn, ffn_w_gate, ffn_w_up, ffn_w_down,
          a_w_in, a_ln_g, a_ln_b, a_w_spatial, a_b_spatial, a_w_out, b_w_in, b_q_norm, b_w_out):
    h = x
    a_v = []
    branches, kv_state = None, None
    for l in range(DEPTH):
        xn = rmsnorm(h, norm_mix[l])
        if l < N_A_LAYERS:
            y, v = chunk_gmlp(xn, a_w_in[l], a_ln_g[l], a_ln_b[l], a_w_spatial[l], a_b_spatial[l], a_w_out[l])
            a_v.append(v)
        else:
            if branches is None:
                branches, kv_state = make_branches(h)
            j = l - N_A_LAYERS
            y = nsa_mixer(xn, pos, b_w_in[j], b_q_norm[j], b_w_out[j], branches)
        h = h + y
        h = h + swiglu(rmsnorm(h, norm_ffn[l]), ffn_w_gate[l], ffn_w_up[l], ffn_w_down[l])
    return h, jnp.stack(a_v), kv_state


def setup_inputs(seed: int = 0) -> dict:
    key = jax.random.key(seed)
    k = jax.random.split(key, 32)
    f32 = jnp.float32

    def nrm(kk, shape, scale):
        return jax.random.normal(kk, shape, f32) * scale

    def gain(kk, shape):
        return 1.0 + 0.05 * jax.random.normal(kk, shape, f32)

    n_pages = PAST_LEN // PAGE_SIZE
    n_used = DEC_BATCH * n_pages
    n_pool = (n_used * 5) // 4
    win_buf = min(WINDOW, PAST_LEN)
    kv_row = (2, N_KV_HEADS, HEAD_DIM)
    d_in_b = N_HEADS * HEAD_DIM + N_BRANCH * N_HEADS
    page_table = jax.random.permutation(k[5], n_pool)[:n_used].reshape(DEC_BATCH, n_pages).astype(jnp.int32)
    return {
        'x_prompt': nrm(k[0], (BATCH, SEQ, D_MODEL), 1.0),
        'x_sample': nrm(k[1], (DEC_BATCH, DEC_SEQ, D_MODEL), 1.0),
        'cache_kv_cmp': nrm(k[2], (n_pool, PAGE_SIZE) + kv_row, 1.0),
        'cache_kv_slc': nrm(k[3], (n_pool, PAGE_SIZE) + kv_row, 1.0),
        'state_kv_win': nrm(k[4], (DEC_BATCH, win_buf) + kv_row, 1.0),
        'page_table': page_table,
        'norm_mix': gain(k[6], (DEPTH, D_MODEL)),
        'norm_ffn': gain(k[7], (DEPTH, D_MODEL)),
        'ffn_w_gate': nrm(k[8], (DEPTH, D_MODEL, D_FF), D_MODEL ** -0.5),
        'ffn_w_up': nrm(k[9], (DEPTH, D_MODEL, D_FF), D_MODEL ** -0.5),
        'ffn_w_down': nrm(k[10], (DEPTH, D_FF, D_MODEL), D_FF ** -0.5),
        'a_w_in': nrm(k[11], (N_A_LAYERS, D_MODEL, 2 * D_A), D_MODEL ** -0.5),
        'a_ln_g': gain(k[12], (N_A_LAYERS, D_A)),
        'a_ln_b': nrm(k[13], (N_A_LAYERS, D_A), 0.02),
        'a_w_spatial': nrm(k[14], (N_A_LAYERS, A_GROUPS, CHUNK, CHUNK), 0.5 * CHUNK ** -0.5),
        'a_b_spatial': gain(k[15], (N_A_LAYERS, A_GROUPS, CHUNK)),
        'a_w_out': nrm(k[16], (N_A_LAYERS, D_A, D_MODEL), D_A ** -0.5),
        'kv_norm': gain(k[17], (D_MODEL,)),
        'kv_w': nrm(k[18], (D_MODEL, N_BRANCH * 2 * N_KV_HEADS * HEAD_DIM), D_MODEL ** -0.5),
        'k_norm_cmp': gain(k[19], (HEAD_DIM,)),
        'k_norm_slc': gain(k[20], (HEAD_DIM,)),
        'k_norm_win': gain(k[21], (HEAD_DIM,)),
        'cmp_pos_k': nrm(k[22], (CMP_LEN, HEAD_DIM), 0.5),
        'cmp_pos_v': nrm(k[23], (CMP_LEN, HEAD_DIM), 0.5),
        'cmp_w_k': nrm(k[24], (CMP_LEN, HEAD_DIM, HEAD_DIM), (CMP_LEN * HEAD_DIM) ** -0.5),
        'cmp_w_v': nrm(k[25], (CMP_LEN, HEAD_DIM, HEAD_DIM), (CMP_LEN * HEAD_DIM) ** -0.5),
        'b_w_in': nrm(k[26], (N_B_LAYERS, D_MODEL, d_in_b), D_MODEL ** -0.5),
        'b_q_norm': gain(k[27], (N_B_LAYERS, HEAD_DIM)),
        'b_w_out': nrm(k[28], (N_B_LAYERS, N_HEADS * HEAD_DIM, D_MODEL), (N_HEADS * HEAD_DIM) ** -0.5),
    }


def reference(x_prompt, x_sample, cache_kv_cmp, cache_kv_slc, state_kv_win, page_table,
              norm_mix, norm_ffn, ffn_w_gate, ffn_w_up, ffn_w_down,
              a_w_in, a_ln_g, a_ln_b, a_w_spatial, a_b_spatial, a_w_out,
              kv_norm, kv_w, k_norm_cmp, k_norm_slc, k_norm_win,
              cmp_pos_k, cmp_pos_v, cmp_w_k, cmp_w_v,
              b_w_in, b_q_norm, b_w_out):
    layer_w = (norm_mix, norm_ffn, ffn_w_gate, ffn_w_up, ffn_w_down,
               a_w_in, a_ln_g, a_ln_b, a_w_spatial, a_b_spatial, a_w_out, b_w_in, b_q_norm, b_w_out)
    cmp_w = (cmp_pos_k, cmp_pos_v, cmp_w_k, cmp_w_v, k_norm_cmp)
    kv_wts = (kv_norm, kv_w, k_norm_slc, k_norm_win)

    t_p = x_prompt.shape[1]
    pos_p = jnp.arange(t_p)

    def prompt_branches(h):
        kv_cmp, kv_slc, kv_win = kv_project(h, pos_p, *kv_wts)
        kc, vc = compress(kv_cmp, *cmp_w)
        n_sel = t_p // SEL_BLOCK

        def branches(q_n, q_rot):
            o_cmp, imp = cmp_branch(q_n, pos_p, kc, vc)
            o_slc = slc_branch(q_rot, pos_p, sel_importance(imp, n_sel), kv_slc)
            o_win = window_prompt(q_rot, kv_win)
            return o_cmp, o_slc, o_win

        return branches, (kv_cmp, kv_slc, kv_win[:, t_p - min(WINDOW, t_p):])

    y_prompt, _, (kv_cmp_p, kv_slc_p, kv_win_p) = trunk(x_prompt, pos_p, prompt_branches, *layer_w)

    b_s, t_s = x_sample.shape[:2]
    pos_s = PAST_LEN + jnp.arange(t_s)

    def sample_branches(h):
        new_cmp, new_slc, new_win = kv_project(h, pos_s, *kv_wts)
        past_shape = (b_s, page_table.shape[1] * PAGE_SIZE) + cache_kv_cmp.shape[2:]
        full_cmp = pad_rows(jnp.concatenate([cache_kv_cmp[page_table].reshape(past_shape), new_cmp], axis=1), SEL_BLOCK)
        full_slc = pad_rows(jnp.concatenate([cache_kv_slc[page_table].reshape(past_shape), new_slc], axis=1), SEL_BLOCK)
        kc, vc = compress(full_cmp, *cmp_w)
        n_sel = full_slc.shape[1] // SEL_BLOCK
        win_buf = state_kv_win.shape[1]
        kv_win_all = jnp.concatenate([state_kv_win, new_win], axis=1)
        win_pos = PAST_LEN - win_buf + jnp.arange(win_buf + t_s)

        def branches(q_n, q_rot):
            o_cmp, imp = cmp_branch(q_n, pos_s, kc, vc)
            o_slc = slc_branch(q_rot, pos_s, sel_importance(imp, n_sel), full_slc)
            o_win = win_attend(q_rot, pos_s, kv_win_all, win_pos)
            return o_cmp, o_slc, o_win

        return branches, (new_cmp, new_slc, kv_win_all[:, t_s:])

    y_sample, a_v_sample, (kv_cmp_s, kv_slc_s, kv_win_s) = trunk(x_sample, pos_s, sample_branches, *layer_w)

    return (y_prompt, y_sample, kv_cmp_p, kv_cmp_s, kv_slc_p, kv_slc_s, kv_win_p, kv_win_s, a_v_sample)
```

```python
import functools

import jax
import jax.numpy as jnp
from jax import lax
from jax.experimental import pallas as pl
from jax.experimental.pallas import tpu as pltpu

CHUNK = 128
A_GROUPS = 8
HEAD_DIM = 64
N_KV_HEADS = 4
ROPE_DIM = HEAD_DIM // 4
ROPE_THETA = 500000.0
CMP_STRIDE = 16
CMP_LEN = 2 * CMP_STRIDE
SEL_BLOCK = 64
SEL_TOPK = 16
WINDOW = 512
N_BRANCH = 3
EPS = 1e-6
ATTN_SCALE = HEAD_DIM ** -0.5

LANES = 128
VMEM_LIMIT_BYTES = 56 * 1024 * 1024
MXU_DTYPE = jnp.bfloat16
NEG = -1e30
F32 = jnp.float32


def _cparams(n_grid):
    return pltpu.CompilerParams(dimension_semantics=("arbitrary",) * n_grid,
                                vmem_limit_bytes=VMEM_LIMIT_BYTES)


def _const_spec(shape):
    nd = len(shape)
    return pl.BlockSpec(shape, lambda *_: (0,) * nd)


def _dot(a, b):
    return jnp.dot(a, b, preferred_element_type=F32)


def _dot_nt(a, b):
    return lax.dot_general(a, b, (((1,), (1,)), ((), ())), preferred_element_type=F32)


def _split3(x):
    hi = x.astype(MXU_DTYPE)
    r1 = x - hi.astype(F32)
    mid = r1.astype(MXU_DTYPE)
    lo = (r1 - mid.astype(F32)).astype(MXU_DTYPE)
    return hi, mid, lo


def _dot_exact_lhs(a, x):
    hi, mid, lo = _split3(x)
    return _dot(a, hi) + _dot(a, mid) + _dot(a, lo)


def _dot_exact_rhs(x, b):
    hi, mid, lo = _split3(x)
    return _dot(hi, b) + _dot(mid, b) + _dot(lo, b)


def _rms(x, g):
    return x * lax.rsqrt(jnp.mean(x * x, axis=-1, keepdims=True) + EPS) * g


def _headnorm128(x, g):
    lo = lax.broadcasted_iota(jnp.int32, x.shape, 1) < HEAD_DIM
    sq = x * x
    s_lo = jnp.sum(jnp.where(lo, sq, 0.0), axis=-1, keepdims=True)
    s_hi = jnp.sum(jnp.where(lo, 0.0, sq), axis=-1, keepdims=True)
    ms = jnp.where(lo, s_lo, s_hi) * (1.0 / HEAD_DIM)
    return x * lax.rsqrt(ms + EPS) * g


def _rope128(x, cos, s_dn, s_up):
    half = ROPE_DIM // 2
    return x * cos + pltpu.roll(x, half, 1) * s_dn + pltpu.roll(x, LANES - half, 1) * s_up


def _gmlp_kernel(h_ref, g_ref, win_ref, lng_ref, lnb_ref, ws_ref, bs_ref, wout_ref, *rest, emit_v):
    if emit_v:
        o_ref, v_ref, gated = rest
    else:
        o_ref, gated = rest
    tm = h_ref.shape[0]
    d_a = lng_ref.shape[1]
    dg = d_a // A_GROUPS
    x = h_ref[...]
    xn = _rms(x, g_ref[...]).astype(MXU_DTYPE)
    u = jax.nn.gelu(_dot(xn, win_ref[:, :d_a]))
    v = jax.nn.gelu(_dot(xn, win_ref[:, d_a:]))
    vc = v - jnp.mean(v, axis=-1, keepdims=True)
    v = vc * lax.rsqrt(jnp.mean(vc * vc, axis=-1, keepdims=True) + EPS) * lng_ref[...] + lnb_ref[...]
    if emit_v:
        v_ref[...] = v
    vb = v.astype(MXU_DTYPE)
    for c in range(tm // CHUNK):
        r0, r1 = c * CHUNK, (c + 1) * CHUNK
        for g in range(A_GROUPS):
            sv = _dot(ws_ref[g], vb[r0:r1, g * dg:(g + 1) * dg]) + bs_ref[:, g:g + 1]
            gated[r0:r1, g * dg:(g + 1) * dg] = (u[r0:r1, g * dg:(g + 1) * dg] * sv).astype(MXU_DTYPE)
    o_ref[...] = x + _dot(gated[...], wout_ref[...])


def _gmlp_layer(h, g, w_in, ln_g, ln_b, ws_eff, bs_eff, w_out, *, emit_v, tm=256):
    m, d = h.shape
    tm = min(tm, m)
    d_a = ln_g.shape[-1]
    row = pl.BlockSpec((tm, d), lambda i: (i, 0))
    out_shape = [jax.ShapeDtypeStruct((m, d), F32)]
    out_specs = [row]
    if emit_v:
        out_shape.append(jax.ShapeDtypeStruct((m, d_a), F32))
        out_specs.append(pl.BlockSpec((tm, d_a), lambda i: (i, 0)))
    res = pl.pallas_call(
        functools.partial(_gmlp_kernel, emit_v=emit_v),
        grid=(m // tm,),
        in_specs=[row, _const_spec((1, d)), _const_spec(w_in.shape), _const_spec((1, d_a)),
                  _const_spec((1, d_a)), _const_spec(ws_eff.shape), _const_spec(bs_eff.shape),
                  _const_spec(w_out.shape)],
        out_specs=out_specs, out_shape=out_shape,
        scratch_shapes=[pltpu.VMEM((tm, d_a), MXU_DTYPE)],
        compiler_params=_cparams(1), name="gmlp_layer",
    )(h, g.reshape(1, d), w_in, ln_g.reshape(1, d_a), ln_b.reshape(1, d_a), ws_eff, bs_eff, w_out)
    return (res[0], res[1]) if emit_v else (res[0], None)


def _ffn_kernel(*refs, with_proj):
    if with_proj:
        h_ref, a_ref, wo_ref, g_ref, wg_ref, wu_ref, wd_ref, o_ref = refs
        x = h_ref[...] + _dot(a_ref[...], wo_ref[...])
    else:
        h_ref, g_ref, wg_ref, wu_ref, wd_ref, o_ref = refs
        x = h_ref[...]
    xn = _rms(x, g_ref[...]).astype(MXU_DTYPE)
    mid = (jax.nn.silu(_dot(xn, wg_ref[...])) * _dot(xn, wu_ref[...])).astype(MXU_DTYPE)
    o_ref[...] = x + _dot(mid, wd_ref[...])


def _ffn_layer(h, g, wg, wu, wd, attn=None, w_o=None, *, tm=256):
    m, d = h.shape
    tm = min(tm, m)
    row = pl.BlockSpec((tm, d), lambda i: (i, 0))
    with_proj = attn is not None
    args, specs = [h], [row]
    if with_proj:
        args += [attn, w_o]
        specs += [pl.BlockSpec((tm, attn.shape[1]), lambda i: (i, 0)), _const_spec(w_o.shape)]
    args += [g.reshape(1, d), wg, wu, wd]
    specs += [_const_spec((1, d)), _const_spec(wg.shape), _const_spec(wu.shape), _const_spec(wd.shape)]
    return pl.pallas_call(
        functools.partial(_ffn_kernel, with_proj=with_proj),
        grid=(m // tm,), in_specs=specs, out_specs=row,
        out_shape=jax.ShapeDtypeStruct((m, d), F32),
        compiler_params=_cparams(1), name="ffn_layer",
    )(*args)


def _kvproj_kernel(h_ref, g_ref, w_ref, gs_ref, gw_ref, cos_ref, sdn_ref, sup_ref,
                   cmp_ref, slc_ref, win_ref, slc16_ref, win16_ref):
    x = h_ref[...]
    xn = _rms(x, g_ref[...]).astype(MXU_DTYPE)
    kv = _dot(xn, w_ref[...])
    width = 2 * N_KV_HEADS * HEAD_DIM
    kw = N_KV_HEADS * HEAD_DIM
    cmp_ref[...] = kv[:, :width]
    cos, sdn, sup = cos_ref[...], sdn_ref[...], sup_ref[...]
    for base, gk_ref, o_ref, o16_ref in ((width, gs_ref, slc_ref, slc16_ref),
                                         (2 * width, gw_ref, win_ref, win16_ref)):
        for c in range(kw // LANES):
            k = _headnorm128(kv[:, base + c * LANES: base + (c + 1) * LANES], gk_ref[...])
            k = _rope128(k, cos, sdn, sup)
            o_ref[:, c * LANES:(c + 1) * LANES] = k
            o16_ref[:, c * LANES:(c + 1) * LANES] = k.astype(MXU_DTYPE)
        v = kv[:, base + kw: base + width]
        o_ref[:, kw:width] = v
        o16_ref[:, kw:width] = v.astype(MXU_DTYPE)


def _kv_project(h, kv_norm, kv_w, gk_slc, gk_win, tabs, *, tm=256):
    m, d = h.shape
    tm = min(tm, m)
    width = 2 * N_KV_HEADS * HEAD_DIM
    n_tab = tabs[0].shape[0] // tm
    row = pl.BlockSpec((tm, d), lambda i: (i, 0))
    tab = pl.BlockSpec((tm, LANES), lambda i: (i % n_tab, 0))
    orow = pl.BlockSpec((tm, width), lambda i: (i, 0))
    return pl.pallas_call(
        _kvproj_kernel, grid=(m // tm,),
        in_specs=[row, _const_spec((1, d)), _const_spec(kv_w.shape), _const_spec((1, LANES)),
                  _const_spec((1, LANES)), tab, tab, tab],
        out_specs=[orow] * 5,
        out_shape=[jax.ShapeDtypeStruct((m, width), F32)] * 3
                  + [jax.ShapeDtypeStruct((m, width), MXU_DTYPE)] * 2,
        compiler_params=_cparams(1), name="kv_project",
    )(h, kv_norm.reshape(1, d), kv_w, gk_slc, gk_win, *tabs)


def _qproj_kernel(h_ref, g_ref, wq_ref, wgt_ref, gq_ref, cos_ref, sdn_ref, sup_ref,
                  qn_ref, qr_ref, gates_ref):
    x = h_ref[...]
    xn = _rms(x, g_ref[...]).astype(MXU_DTYPE)
    q = _dot(xn, wq_ref[...])
    gates_ref[...] = jax.nn.sigmoid(_dot(xn, wgt_ref[...]))
    cos, sdn, sup = cos_ref[...], sdn_ref[...], sup_ref[...]
    for c in range(q.shape[1] // LANES):
        qn = _headnorm128(q[:, c * LANES:(c + 1) * LANES], gq_ref[...])
        qn_ref[:, c * LANES:(c + 1) * LANES] = (qn * ATTN_SCALE).astype(qn_ref.dtype)
        qr_ref[:, c * LANES:(c + 1) * LANES] = (_rope128(qn, cos, sdn, sup) * ATTN_SCALE).astype(qr_ref.dtype)


def _q_project(h, g, w_q, w_gate, gq, tabs, *, tm=256):
    m, d = h.shape
    tm = min(tm, m)
    dq = w_q.shape[1]
    n_tab = tabs[0].shape[0] // tm
    row = pl.BlockSpec((tm, d), lambda i: (i, 0))
    tab = pl.BlockSpec((tm, LANES), lambda i: (i % n_tab, 0))
    qrow = pl.BlockSpec((tm, dq), lambda i: (i, 0))
    return pl.pallas_call(
        _qproj_kernel, grid=(m // tm,),
        in_specs=[row, _const_spec((1, d)), _const_spec(w_q.shape), _const_spec(w_gate.shape),
                  _const_spec((1, LANES)), tab, tab, tab],
        out_specs=[qrow, qrow, pl.BlockSpec((tm, LANES), lambda i: (i, 0))],
        out_shape=[jax.ShapeDtypeStruct((m, dq), MXU_DTYPE)] * 2 + [jax.ShapeDtypeStruct((m, LANES), F32)],
        compiler_params=_cparams(1), name="q_project",
    )(h, g.reshape(1, d), w_q, w_gate, gq, *tabs)


def _compress_kernel(x_ref, pos_ref, w_ref, o_ref):
    width = 2 * N_KV_HEADS * HEAD_DIM
    kw = N_KV_HEADS * HEAD_DIM
    for kv in range(2):
        acc_a = jnp.zeros((x_ref.shape[0], kw), F32)
        acc_b = jnp.zeros((x_ref.shape[0], kw), F32)
        for l in range(CMP_STRIDE):
            z = x_ref[:, l * width + kv * kw: l * width + (kv + 1) * kw]
            pa = pos_ref[l:l + 1, kv * kw:(kv + 1) * kw]
            pb = pos_ref[CMP_STRIDE + l:CMP_STRIDE + l + 1, kv * kw:(kv + 1) * kw]
            acc_a = acc_a + _dot((z + pa).astype(MXU_DTYPE), w_ref[kv, l])
            acc_b = acc_b + _dot((z + pb).astype(MXU_DTYPE), w_ref[kv, CMP_STRIDE + l])
        o_ref[:, kv * kw:(kv + 1) * kw] = acc_a
        o_ref[:, width + kv * kw: width + (kv + 1) * kw] = acc_b


def _compress_partial(rows, pos_tab, w_bd, *, mb=256):
    n_sub, flat = rows.shape
    mb = min(mb, n_sub)
    width = 2 * N_KV_HEADS * HEAD_DIM
    return pl.pallas_call(
        _compress_kernel, grid=(n_sub // mb,),
        in_specs=[pl.BlockSpec((mb, flat), lambda i: (i, 0)), _const_spec(pos_tab.shape),
                  _const_spec(w_bd.shape)],
        out_specs=pl.BlockSpec((mb, 2 * width), lambda i: (i, 0)),
        out_shape=jax.ShapeDtypeStruct((n_sub, 2 * width), F32),
        compiler_params=_cparams(1), name="compress_partial",
    )(rows, pos_tab, w_bd)


def _assemble_kc_vc(ab, gk):
    width = 2 * N_KV_HEADS * HEAD_DIM
    kw = N_KV_HEADS * HEAD_DIM
    n = ab.shape[0]
    kcv = ab[:, :width] + pltpu.roll(ab[:, width:], n - 1, 0)
    kc = jnp.concatenate([_headnorm128(kcv[:, c * LANES:(c + 1) * LANES], gk) for c in range(kw // LANES)],
                         axis=1)
    return kc, kcv[:, kw:width]


def _kcvc_kernel(ab_ref, gk_ref, kc_ref, vc_ref):
    kc, vc = _assemble_kc_vc(ab_ref[0], gk_ref[...])
    kc_ref[0] = kc.astype(kc_ref.dtype)
    vc_ref[0] = vc.astype(vc_ref.dtype)


def _prompt_kc_vc(ab, gk):
    b, n, w = ab.shape
    kw = N_KV_HEADS * HEAD_DIM
    spec = pl.BlockSpec((1, n, kw), lambda i: (i, 0, 0))
    return pl.pallas_call(
        _kcvc_kernel, grid=(b,),
        in_specs=[pl.BlockSpec((1, n, w), lambda i: (i, 0, 0)), _const_spec((1, LANES))],
        out_specs=[spec, spec], out_shape=[jax.ShapeDtypeStruct((b, n, kw), MXU_DTYPE)] * 2,
        compiler_params=_cparams(1), name="prompt_kc_vc",
    )(ab, gk)


def _select_blocks(imp, blk, qpos, axis, n_sel):
    cur = qpos // SEL_BLOCK
    valid = blk <= cur
    forced = (blk == 0) | (blk == cur) | (blk == cur - 1)
    sc = jnp.where(forced, jnp.inf, jnp.where(valid, imp, -jnp.inf))
    cnt = jnp.zeros(sc.shape, jnp.int32)
    for j in range(n_sel):
        sj = lax.slice_in_dim(sc, j, j + 1, axis=axis)
        beats = (sj > sc) | ((sj == sc) & (blk > j))
        cnt = cnt + beats.astype(jnp.int32)
    return (cnt < min(SEL_TOPK, n_sel)) & (sc > -jnp.inf) & (blk < n_sel)


def _attn_prompt_kernel(qn_ref, qr_ref, gt_ref, kc_ref, vc_ref, slc_ref, win_ref, e_ref, mt_ref, o_ref,
                        allowed, m_s, l_s, acc_s, *, n_sel):
    tq = qn_ref.shape[0]
    tk = tq
    n_kt = slc_ref.shape[1] // tk
    kw = N_KV_HEADS * HEAD_DIM
    qb = pl.program_id(1)
    lane = lax.broadcasted_iota(jnp.int32, (tq, LANES), 1)
    row = lax.broadcasted_iota(jnp.int32, (tq, LANES), 0)
    lo = lane < HEAD_DIM
    qpos = qb * tq + row
    rpg = 4

    def stack(ref, p):
        parts = []
        for side in range(2):
            keep = lo if side == 0 else jnp.logical_not(lo)
            for r in range(rpg):
                c = p * rpg + r
                t = ref[:, c * LANES:(c + 1) * LANES]
                parts.append(jnp.where(keep, t, jnp.zeros_like(t)))
        return jnp.concatenate(parts, axis=0)

    def flash(q_st, kv_ref, p, kt_lo, kt_hi, mask_fn):
        m_s[...] = jnp.full(m_s.shape, NEG, F32)
        l_s[...] = jnp.zeros(l_s.shape, F32)
        acc_s[...] = jnp.zeros(acc_s.shape, F32)

        def body(kt, carry):
            k0 = pl.multiple_of(kt * tk, tk)
            k = kv_ref[0, pl.ds(k0, tk), p * LANES:(p + 1) * LANES]
            v = kv_ref[0, pl.ds(k0, tk), kw + p * LANES: kw + (p + 1) * LANES]
            s = _dot_nt(q_st, k).reshape(2 * rpg, tq, tk)
            s = jnp.where(mask_fn(kt, k0), s, NEG)
            m_prev = m_s[...]
            m_new = jnp.maximum(m_prev, jnp.max(s, axis=-1, keepdims=True))
            alpha = jnp.exp(m_prev - m_new)
            pe = jnp.exp(s - m_new)
            l_s[...] = alpha * l_s[...] + jnp.sum(pe, axis=-1, keepdims=True)
            pv = _dot(pe.reshape(2 * rpg * tq, tk).astype(MXU_DTYPE), v)
            acc_s[...] = alpha * acc_s[...] + pv.reshape(2 * rpg, tq, LANES)
            m_s[...] = m_new
            return carry

        lax.fori_loop(kt_lo, kt_hi, body, 0)
        out = acc_s[...] / jnp.maximum(l_s[...], 1e-30)
        return [jnp.where(lo, out[r], out[rpg + r]) for r in range(rpg)]

    for p in range(2):
        qn_st = stack(qn_ref, p)
        s = _dot_nt(qn_st, kc_ref[0, :, p * LANES:(p + 1) * LANES]).reshape(2 * rpg, tq, LANES)
        cmask = ((lane * CMP_STRIDE + (CMP_LEN - 1)) <= qpos)[None]
        s = jnp.where(cmask, s, -jnp.inf)
        mx = jnp.max(s, axis=-1, keepdims=True)
        mx = jnp.where(mx == -jnp.inf, 0.0, mx)
        e = jnp.where(cmask, jnp.exp(s - mx), 0.0)
        pr = e / jnp.maximum(jnp.sum(e, axis=-1, keepdims=True), 1e-30)
        oc = _dot(pr.reshape(2 * rpg * tq, LANES).astype(MXU_DTYPE),
                  vc_ref[0, :, p * LANES:(p + 1) * LANES]).reshape(2 * rpg, tq, LANES)
        o_cmp = [jnp.where(lo, oc[r], oc[rpg + r]) for r in range(rpg)]

        for side in range(2):
            g = 2 * p + side
            imp = pr[side * rpg] + pr[side * rpg + 1] + pr[side * rpg + 2] + pr[side * rpg + 3]
            imp_t = _dot_exact_lhs(mt_ref[...], imp.T)[:n_sel]
            blk = lax.broadcasted_iota(jnp.int32, (n_sel, tq), 0)
            qp_t = qb * tq + lax.broadcasted_iota(jnp.int32, (n_sel, tq), 1)
            sel = _select_blocks(imp_t, blk, qp_t, 0, n_sel)
            sel_f = jnp.concatenate([jnp.where(sel, 1.0, 0.0).astype(F32),
                                     jnp.zeros((LANES - n_sel, tq), F32)], axis=0).T
            al = _dot(sel_f.astype(MXU_DTYPE), e_ref[...])
            for kt in range(n_kt):
                kpos = kt * tk + lane
                ok = (al[:, kt * tk:(kt + 1) * tk] > 0.5) & (kpos <= qpos)
                allowed[g, kt] = jnp.where(ok, 1.0, 0.0)

        def slc_mask(kt, k0, p=p):
            a0 = jnp.broadcast_to((allowed[2 * p, kt] > 0.5)[None], (rpg, tq, tk))
            a1 = jnp.broadcast_to((allowed[2 * p + 1, kt] > 0.5)[None], (rpg, tq, tk))
            return jnp.concatenate([a0, a1], axis=0)

        qr_st = stack(qr_ref, p)
        o_slc = flash(qr_st, slc_ref, p, 0, qb + 1, slc_mask)

        def win_mask(kt, k0):
            kpos = k0 + lane
            return ((kpos <= qpos) & (kpos > qpos - WINDOW))[None]

        o_win = flash(qr_st, win_ref, p, jnp.maximum(qb - WINDOW // tk, 0), qb + 1, win_mask)

        gt = gt_ref[...]
        for r in range(rpg):
            c = p * rpg + r
            h_lo = (2 * p) * rpg + r
            h_hi = (2 * p + 1) * rpg + r

            def gate(br):
                return jnp.where(lo, gt[:, h_lo * N_BRANCH + br: h_lo * N_BRANCH + br + 1],
                                 gt[:, h_hi * N_BRANCH + br: h_hi * N_BRANCH + br + 1])

            o = gate(0) * o_cmp[r] + gate(1) * o_slc[r] + gate(2) * o_win[r]
            o_ref[:, c * LANES:(c + 1) * LANES] = o.astype(o_ref.dtype)


def _attn_prompt(qn, qr, gates, kc, vc, slc16, win16, e_mat, mt, *, n_sel, tq=128):
    b, t, w = slc16.shape
    nq = t // tq
    dq = qn.shape[1]
    n_cb = kc.shape[1]
    qspec = pl.BlockSpec((tq, dq), lambda i, j: (i * nq + j, 0))
    seq = lambda shape: pl.BlockSpec(shape, lambda i, j: (i, 0, 0))
    return pl.pallas_call(
        functools.partial(_attn_prompt_kernel, n_sel=n_sel),
        grid=(b, nq),
        in_specs=[qspec, qspec, pl.BlockSpec((tq, LANES), lambda i, j: (i * nq + j, 0)),
                  seq((1, n_cb, kc.shape[2])), seq((1, n_cb, vc.shape[2])),
                  seq((1, t, w)), seq((1, t, w)), _const_spec(e_mat.shape), _const_spec(mt.shape)],
        out_specs=qspec, out_shape=jax.ShapeDtypeStruct((b * t, dq), MXU_DTYPE),
        scratch_shapes=[pltpu.VMEM((N_KV_HEADS, t // tq, tq, tq), F32),
                        pltpu.VMEM((8, tq, 1), F32), pltpu.VMEM((8, tq, 1), F32),
                        pltpu.VMEM((8, tq, LANES), F32)],
        compiler_params=_cparams(2), name="attn_prompt",
    )(qn, qr, gates, kc, vc, slc16, win16, e_mat, mt)


def _attn_sample_kernel(pt_ref, qn_ref, qr_ref, gt_ref, gk_ref, rsum_ref, m_ref, e_ref, abn_ref, sn_ref,
                        wst_ref, wn_ref, *rest, n_pages, n_sel, past_len, t_new):
    ab_pages = rest[:n_pages]
    slc_pages = rest[n_pages:2 * n_pages]
    o_ref, wout_ref, pad_s = rest[2 * n_pages:]
    kw = N_KV_HEADS * HEAD_DIM
    page = slc_pages[0].shape[1]
    nrow = qn_ref.shape[1]
    rows_per_g = nrow // N_KV_HEADS
    qn = qn_ref[0]
    qr = qr_ref[0]
    t_of_row = lax.broadcasted_iota(jnp.int32, (nrow, 1), 0) % t_new
    qpos = past_len + t_of_row

    def softmax(s, mask):
        s = jnp.where(mask, s, -jnp.inf)
        mx = jnp.max(s, axis=-1, keepdims=True)
        mx = jnp.where(mx == -jnp.inf, 0.0, mx)
        e = jnp.where(mask, jnp.exp(s - mx), 0.0)
        return e / jnp.maximum(jnp.sum(e, axis=-1, keepdims=True), 1e-30)

    n_ab = m_ref.shape[0]
    pieces = [r[0] for r in ab_pages] + [abn_ref[0]]
    n_have = sum(x.shape[0] for x in pieces)
    pieces.append(jnp.zeros((n_ab - n_have, pieces[0].shape[1]), F32))
    kc, vc = _assemble_kc_vc(jnp.concatenate(pieces, axis=0), gk_ref[...])
    blk_c = lax.broadcasted_iota(jnp.int32, (nrow, n_ab), 1)
    cmask = (blk_c * CMP_STRIDE + (CMP_LEN - 1)) <= qpos
    pc = softmax(_dot_nt(qn, kc.astype(MXU_DTYPE)), cmask)
    o_cmp = _dot(pc.astype(MXU_DTYPE), vc.astype(MXU_DTYPE))

    imp = _dot_exact_lhs(rsum_ref[...], pc)
    imp_s = _dot_exact_rhs(imp, m_ref[...])
    blk = lax.broadcasted_iota(jnp.int32, imp_s.shape, 1)
    sel = _select_blocks(imp_s, blk, jnp.broadcast_to(qpos, imp_s.shape), 1, n_sel)
    al = _dot(jnp.where(sel, 1.0, 0.0).astype(MXU_DTYPE), e_ref[...])
    n_keys = al.shape[1]
    kpos = lax.broadcasted_iota(jnp.int32, (nrow, n_keys), 1)
    smask = (al > 0.5) & (kpos <= qpos)

    pad_s[...] = jnp.zeros(pad_s.shape, F32)
    pad_s[0:sn_ref.shape[1], :] = sn_ref[0]
    tiles = [r[0] for r in slc_pages] + [pad_s[...]]
    s = jnp.concatenate([_dot_nt(qr, t[:, :kw].astype(MXU_DTYPE)) for t in tiles], axis=1)
    ps = softmax(s, smask).astype(MXU_DTYPE)
    o_slc = jnp.zeros((nrow, kw), F32)
    for i, t in enumerate(tiles):
        o_slc = o_slc + _dot(ps[:, i * page:(i + 1) * page], t[:, kw:].astype(MXU_DTYPE))

    wbuf = wst_ref.shape[1]
    pad_s[...] = jnp.zeros(pad_s.shape, F32)
    pad_s[0:wn_ref.shape[1], :] = wn_ref[0]
    wst = wst_ref[0]
    wtiles = [wst[i * page:(i + 1) * page] for i in range(wbuf // page)] + [pad_s[...]]
    s = jnp.concatenate([_dot_nt(qr, t[:, :kw].astype(MXU_DTYPE)) for t in wtiles], axis=1)
    wpos = (past_len - wbuf) + lax.broadcasted_iota(jnp.int32, s.shape, 1)
    wmask = (wpos <= qpos) & (wpos > qpos - WINDOW) & (wpos >= 0)
    pw = softmax(s, wmask).astype(MXU_DTYPE)
    o_win = jnp.zeros((nrow, kw), F32)
    for i, t in enumerate(wtiles):
        o_win = o_win + _dot(pw[:, i * page:(i + 1) * page], t[:, kw:].astype(MXU_DTYPE))

    gt = gt_ref[0]
    o = gt[:, 0:1] * o_cmp + gt[:, 1:2] * o_slc + gt[:, 2:3] * o_win
    lane_g = lax.broadcasted_iota(jnp.int32, (rows_per_g, kw), 1) // HEAD_DIM
    acc = jnp.zeros((rows_per_g, kw), F32)
    for g in range(N_KV_HEADS):
        acc = acc + jnp.where(lane_g == g, o[g * rows_per_g:(g + 1) * rows_per_g], 0.0)
    o_ref[0] = acc

    rolled = pltpu.roll(wst, wbuf - t_new, 0)
    wout_ref[0, 0:wbuf - 8, :] = rolled[0:wbuf - 8]
    tail_new = pltpu.roll(wn_ref[0], 8 - t_new, 0)
    r8 = lax.broadcasted_iota(jnp.int32, tail_new.shape, 0)
    wout_ref[0, wbuf - 8:wbuf, :] = jnp.where(r8 >= 8 - t_new, tail_new, rolled[wbuf - 8:wbuf])


def _attn_sample(page_table, qn, qr, gates, gk, rsum, m_mat, e_mat, ab_pool, ab_new, cache_slc, slc_new,
                 win_state, win_new, *, n_sel, past_len, t_new):
    nb, n_pages = page_table.shape
    nrow, kw = qn.shape[1], qn.shape[2]
    page = cache_slc.shape[1]
    width = cache_slc.shape[2]
    wbuf = win_state.shape[1]
    per_b = lambda shape: pl.BlockSpec(shape, lambda i, pt: (i, 0, 0))

    def page_spec(shape, p):
        return pl.BlockSpec(shape, lambda i, pt: (pt[i * n_pages + p], 0, 0))

    in_specs = [per_b((1, nrow, kw)), per_b((1, nrow, kw)), per_b((1, nrow, N_BRANCH)),
                pl.BlockSpec((1, LANES), lambda i, pt: (0, 0)),
                pl.BlockSpec(rsum.shape, lambda i, pt: (0, 0)),
                pl.BlockSpec(m_mat.shape, lambda i, pt: (0, 0)),
                pl.BlockSpec(e_mat.shape, lambda i, pt: (0, 0)),
                per_b((1,) + ab_new.shape[1:]), per_b((1,) + slc_new.shape[1:]),
                per_b((1, wbuf, width)), per_b((1,) + win_new.shape[1:])]
    in_specs += [page_spec((1,) + ab_pool.shape[1:], p) for p in range(n_pages)]
    in_specs += [page_spec((1, page, width), p) for p in range(n_pages)]
    grid_spec = pltpu.PrefetchScalarGridSpec(
        num_scalar_prefetch=1, grid=(nb,), in_specs=in_specs,
        out_specs=[per_b((1, nrow // N_KV_HEADS, kw)), per_b((1, wbuf, width))],
        scratch_shapes=[pltpu.VMEM((page, width), F32)])
    return pl.pallas_call(
        functools.partial(_attn_sample_kernel, n_pages=n_pages, n_sel=n_sel, past_len=past_len, t_new=t_new),
        grid_spec=grid_spec,
        out_shape=[jax.ShapeDtypeStruct((nb, nrow // N_KV_HEADS, kw), F32),
                   jax.ShapeDtypeStruct((nb, wbuf, width), F32)],
        compiler_params=_cparams(1), name="attn_sample",
    )(page_table.reshape(-1), qn, qr, gates, gk, rsum, m_mat, e_mat, ab_new, slc_new, win_state, win_new,
      *([ab_pool] * n_pages), *([cache_slc] * n_pages))


def _rope_tables(pos):
    half = ROPE_DIM // 2
    inv_freq = ROPE_THETA ** (-jnp.arange(half, dtype=F32) / half)
    ang = pos.astype(F32)[:, None] * inv_freq[None, :]
    cos, sin = jnp.cos(ang), jnp.sin(ang)
    n = pos.shape[0]
    z = jnp.zeros((n, HEAD_DIM - ROPE_DIM), F32)
    zh = jnp.zeros((n, half), F32)
    c_head = jnp.concatenate([cos, cos, z + 1.0], axis=1)
    dn_head = jnp.concatenate([zh, sin, z], axis=1)
    up_head = jnp.concatenate([-sin, zh, z], axis=1)
    tile2 = lambda a: jnp.concatenate([a, a], axis=1)
    return tile2(c_head), tile2(dn_head), tile2(up_head)


def _sel_matrices(n_cmp, n_cmp_pad, n_sel, n_keys):
    i = jnp.arange(n_cmp_pad)[:, None]
    j = jnp.arange(LANES)[None, :]
    m = ((i * CMP_STRIDE < j * SEL_BLOCK + SEL_BLOCK) & (i * CMP_STRIDE + CMP_LEN > j * SEL_BLOCK)
         & (i < n_cmp) & (j < n_sel))
    e = (jnp.arange(n_keys)[None, :] // SEL_BLOCK == jnp.arange(LANES)[:, None]) & (jnp.arange(LANES)[:, None] < n_sel)
    return m.astype(MXU_DTYPE), e.astype(MXU_DTYPE)


def _tile2(g):
    return jnp.concatenate([g, g]).reshape(1, LANES).astype(F32)


def kernel(x_prompt, x_sample, cache_kv_cmp, cache_kv_slc, state_kv_win, page_table, norm_mix, norm_ffn,
           ffn_w_gate, ffn_w_up, ffn_w_down, a_w_in, a_ln_g, a_ln_b, a_w_spatial, a_b_spatial, a_w_out,
           kv_norm, kv_w, k_norm_cmp, k_norm_slc, k_norm_win, cmp_pos_k, cmp_pos_v, cmp_w_k, cmp_w_v,
           b_w_in, b_q_norm, b_w_out):
    bf = MXU_DTYPE
    depth = norm_mix.shape[0]
    n_a = a_w_in.shape[0]
    bsz, t_p, d_model = x_prompt.shape
    nb, t_s, _ = x_sample.shape
    n_pages = page_table.shape[1]
    page = cache_kv_cmp.shape[1]
    past_len = n_pages * page
    width = 2 * N_KV_HEADS * HEAD_DIM
    kw = N_KV_HEADS * HEAD_DIM
    n_heads = b_w_out.shape[1] // HEAD_DIM
    rpg = n_heads // N_KV_HEADS
    dq = n_heads * HEAD_DIM
    assert N_KV_HEADS == 4 and rpg == 4 and t_p % CHUNK == 0 and CHUNK % t_s == 0 and t_s <= 8

    wg, wu, wd = ffn_w_gate.astype(bf), ffn_w_up.astype(bf), ffn_w_down.astype(bf)
    a_in, a_out = a_w_in.astype(bf), a_w_out.astype(bf)
    tril = jnp.tril(jnp.ones((CHUNK, CHUNK), bool))
    ws_p = jnp.where(tril[None, None], a_w_spatial, 0.0).astype(bf)
    bs_p = jnp.swapaxes(a_b_spatial, 1, 2)
    rep = CHUNK // t_s
    ws4 = jnp.where(tril[None, None, :t_s, :t_s], a_w_spatial[:, :, :t_s, :t_s], 0.0)
    ws_s = jnp.einsum('ab,lgts->lgatbs', jnp.eye(rep, dtype=F32), ws4).reshape(n_a, A_GROUPS, CHUNK, CHUNK).astype(bf)
    bs_s = jnp.tile(jnp.swapaxes(a_b_spatial[:, :, :t_s], 1, 2), (1, rep, 1))
    perm = jnp.array([(2 * p + side) * rpg + r for p in range(2) for r in range(rpg) for side in range(2)])
    wq = b_w_in[:, :, :dq].reshape(-1, d_model, n_heads, HEAD_DIM)[:, :, perm].reshape(-1, d_model, dq).astype(bf)
    wgate = jnp.pad(b_w_in[:, :, dq:], ((0, 0), (0, 0), (0, LANES - N_BRANCH * n_heads))).astype(bf)
    wo_pair = b_w_out.reshape(-1, n_heads, HEAD_DIM, d_model)[:, perm].reshape(-1, dq, d_model).astype(bf)
    wo_flat = b_w_out.astype(bf)
    kvw = kv_w.astype(bf)
    eye_g = jnp.eye(N_KV_HEADS, dtype=F32)
    w_bd = jnp.stack([jnp.einsum('ab,lde->ladbe', eye_g, w).reshape(CMP_LEN, kw, kw)
                      for w in (cmp_w_k, cmp_w_v)]).astype(bf)
    pos_tab = jnp.concatenate([jnp.tile(cmp_pos_k, (1, N_KV_HEADS)), jnp.tile(cmp_pos_v, (1, N_KV_HEADS))], axis=1)
    gk_cmp, gk_slc, gk_win = _tile2(k_norm_cmp), _tile2(k_norm_slc), _tile2(k_norm_win)

    def dense_a_layers(h, ws_eff, bs_eff, emit_v):
        vs = []
        for l in range(n_a):
            h, v = _gmlp_layer(h, norm_mix[l], a_in[l], a_ln_g[l], a_ln_b[l], ws_eff[l], bs_eff[l], a_out[l],
                               emit_v=emit_v)
            vs.append(v)
            h = _ffn_layer(h, norm_ffn[l], wg[l], wu[l], wd[l])
        return h, vs

    tabs_p = _rope_tables(jnp.arange(t_p))
    h = x_prompt.reshape(bsz * t_p, d_model)
    h, _ = dense_a_layers(h, ws_p, bs_p, False)
    kv_cmp_p, kv_slc_p, kv_win_p, slc16, win16 = _kv_project(h, kv_norm, kvw, gk_slc, gk_win, tabs_p)
    ab = _compress_partial(kv_cmp_p.reshape(bsz * t_p // CMP_STRIDE, CMP_STRIDE * width), pos_tab, w_bd)
    n_sub = t_p // CMP_STRIDE
    kc, vc = _prompt_kc_vc(ab.reshape(bsz, n_sub, 2 * width), gk_cmp)
    n_sel_p = t_p // SEL_BLOCK
    m_p, e_p = _sel_matrices(n_sub - 1, n_sub, n_sel_p, t_p)
    for j in range(depth - n_a):
        l = n_a + j
        qn, qr, gates = _q_project(h, norm_mix[l], wq[j], wgate[j], _tile2(b_q_norm[j]), tabs_p)
        o = _attn_prompt(qn, qr, gates, kc, vc, slc16.reshape(bsz, t_p, width), win16.reshape(bsz, t_p, width),
                         e_p, m_p.T, n_sel=n_sel_p)
        h = _ffn_layer(h, norm_ffn[l], wg[l], wu[l], wd[l], attn=o, w_o=wo_pair[j])
    y_prompt = h.reshape(bsz, t_p, d_model)
    row5 = lambda a, b_, t_: a.reshape(b_, t_, 2, N_KV_HEADS, HEAD_DIM)
    w_keep = min(WINDOW, t_p)
    kv_win_prompt = row5(kv_win_p, bsz, t_p)[:, t_p - w_keep:]

    m_s = nb * t_s
    tabs_s = _rope_tables(past_len + (jnp.arange(nb * t_s) % t_s))
    hs = x_sample.reshape(m_s, d_model)
    hs, vs = dense_a_layers(hs, ws_s, bs_s, True)
    a_v_sample = jnp.stack(vs).reshape(n_a, nb, t_s, -1)
    kv_cmp_s, kv_slc_s, kv_win_s, _, _ = _kv_project(hs, kv_norm, kvw, gk_slc, gk_win, tabs_s)
    n_pool = cache_kv_cmp.shape[0]
    sub_pp = page // CMP_STRIDE
    ab_pool = _compress_partial(cache_kv_cmp.reshape(n_pool * sub_pp, CMP_STRIDE * width), pos_tab, w_bd)
    ab_pool = ab_pool.reshape(n_pool, sub_pp, 2 * width)
    t_full = -(-(past_len + t_s) // SEL_BLOCK) * SEL_BLOCK
    n_new_sub = (t_full - past_len) // CMP_STRIDE
    new_cmp = jnp.pad(kv_cmp_s.reshape(nb, t_s, width), ((0, 0), (0, t_full - past_len - t_s), (0, 0)))
    ab_new = _compress_partial(new_cmp.reshape(nb * n_new_sub, CMP_STRIDE * width), pos_tab, w_bd)
    ab_new = jnp.pad(ab_new.reshape(nb, n_new_sub, 2 * width), ((0, 0), (0, 8 - n_new_sub), (0, 0)))
    n_cmp_s = t_full // CMP_STRIDE - 1
    n_sel_s = t_full // SEL_BLOCK
    n_cmp_pad = 256
    n_keys = past_len + page
    m_sm, e_sm = _sel_matrices(n_cmp_s, n_cmp_pad, n_sel_s, n_keys)
    nrow = n_heads * t_s
    ridx = jnp.arange(nrow)
    rsum = ((ridx[:, None] // (rpg * t_s) == ridx[None, :] // (rpg * t_s))
            & (ridx[:, None] % t_s == ridx[None, :] % t_s)).astype(bf)
    pad8 = lambda a: jnp.pad(a.reshape(nb, t_s, width), ((0, 0), (0, 8 - t_s), (0, 0)))
    slc_new, win_new = pad8(kv_slc_s), pad8(kv_win_s)
    inv_perm = jnp.argsort(perm)

    def pad_q(q):
        q = q.reshape(nb, t_s, n_heads, HEAD_DIM)[:, :, inv_perm].reshape(nb, t_s, N_KV_HEADS, rpg, HEAD_DIM)
        q = jnp.transpose(q, (0, 2, 3, 1, 4))
        q = jnp.einsum('bgrtd,gk->bgrtkd', q, jnp.eye(N_KV_HEADS, dtype=q.dtype))
        return q.reshape(nb, nrow, kw)

    kv_win_sample = None
    for j in range(depth - n_a):
        l = n_a + j
        qn, qr, gates = _q_project(hs, norm_mix[l], wq[j], wgate[j], _tile2(b_q_norm[j]), tabs_s)
        g3 = gates[:, :N_BRANCH * n_heads].reshape(nb, t_s, N_KV_HEADS, rpg, N_BRANCH)
        g3 = jnp.transpose(g3, (0, 2, 3, 1, 4)).reshape(nb, nrow, N_BRANCH)
        o, kv_win_sample = _attn_sample(page_table, pad_q(qn), pad_q(qr), g3, gk_cmp, rsum, m_sm, e_sm,
                                        ab_pool, ab_new, cache_kv_slc.reshape(n_pool, page, width), slc_new,
                                        state_kv_win.reshape(nb, -1, width), win_new,
                                        n_sel=n_sel_s, past_len=past_len, t_new=t_s)
        o = jnp.transpose(o.reshape(nb, rpg, t_s, N_KV_HEADS, HEAD_DIM), (0, 2, 3, 1, 4)).reshape(m_s, dq)
        hs = _ffn_layer(hs, norm_ffn[l], wg[l], wu[l], wd[l], attn=o.astype(bf), w_o=wo_flat[j])
    y_sample = hs.reshape(nb, t_s, d_model)

    return (y_prompt, y_sample, row5(kv_cmp_p, bsz, t_p), row5(kv_cmp_s, nb, t_s),
            row5(kv_slc_p, bsz, t_p), row5(kv_slc_s, nb, t_s), kv_win_prompt,
            kv_win_sample.reshape(state_kv_win.shape), a_v_sample)
```

```python
import functools

import jax
import jax.numpy as jnp
from jax import lax
from jax.experimental import pallas as pl
from jax.experimental.pallas import tpu as pltpu

CHUNK = 128
A_GROUPS = 8
HEAD_DIM = 64
N_KV_HEADS = 4
ROPE_DIM = HEAD_DIM // 4
ROPE_THETA = 500000.0
CMP_STRIDE = 16
CMP_LEN = 2 * CMP_STRIDE
SEL_BLOCK = 64
SEL_TOPK = 16
WINDOW = 512
N_BRANCH = 3
EPS = 1e-6
ATTN_SCALE = HEAD_DIM ** -0.5

LANES = 128
VMEM_LIMIT_BYTES = 56 * 1024 * 1024
MXU_DTYPE = jnp.bfloat16
NEG = -1e30
F32 = jnp.float32


def _cparams(n_grid):
    return pltpu.CompilerParams(dimension_semantics=("arbitrary",) * n_grid,
                                vmem_limit_bytes=VMEM_LIMIT_BYTES)


def _const_spec(shape):
    nd = len(shape)
    return pl.BlockSpec(shape, lambda *_: (0,) * nd)


def _dot(a, b):
    return jnp.dot(a, b, preferred_element_type=F32)


def _dot_nt(a, b):
    return lax.dot_general(a, b, (((1,), (1,)), ((), ())), preferred_element_type=F32)


def _split3(x):
    hi = x.astype(MXU_DTYPE)
    r1 = x - hi.astype(F32)
    mid = r1.astype(MXU_DTYPE)
    lo = (r1 - mid.astype(F32)).astype(MXU_DTYPE)
    return hi, mid, lo


def _dot_exact_lhs(a, x):
    hi, mid, lo = _split3(x)
    return _dot(a, hi) + _dot(a, mid) + _dot(a, lo)


def _dot_exact_rhs(x, b):
    hi, mid, lo = _split3(x)
    return _dot(hi, b) + _dot(mid, b) + _dot(lo, b)


def _rms(x, g):
    return x * lax.rsqrt(jnp.mean(x * x, axis=-1, keepdims=True) + EPS) * g


def _headnorm128(x, g):
    lo = lax.broadcasted_iota(jnp.int32, x.shape, 1) < HEAD_DIM
    sq = x * x
    s_lo = jnp.sum(jnp.where(lo, sq, 0.0), axis=-1, keepdims=True)
    s_hi = jnp.sum(jnp.where(lo, 0.0, sq), axis=-1, keepdims=True)
    ms = jnp.where(lo, s_lo, s_hi) * (1.0 / HEAD_DIM)
    return x * lax.rsqrt(ms + EPS) * g


def _rope128(x, cos, s_dn, s_up):
    half = ROPE_DIM // 2
    return x * cos + pltpu.roll(x, half, 1) * s_dn + pltpu.roll(x, LANES - half, 1) * s_up


def _gmlp_kernel(h_ref, g_ref, win_ref, lng_ref, lnb_ref, ws_ref, bs_ref, wout_ref, *rest, emit_v):
    if emit_v:
        o_ref, v_ref, gated = rest
    else:
        o_ref, gated = rest
    tm = h_ref.shape[0]
    d_a = lng_ref.shape[1]
    dg = d_a // A_GROUPS
    x = h_ref[...]
    xn = _rms(x, g_ref[...]).astype(MXU_DTYPE)
    u = jax.nn.gelu(_dot(xn, win_ref[:, :d_a]))
    v = jax.nn.gelu(_dot(xn, win_ref[:, d_a:]))
    vc = v - jnp.mean(v, axis=-1, keepdims=True)
    v = vc * lax.rsqrt(jnp.mean(vc * vc, axis=-1, keepdims=True) + EPS) * lng_ref[...] + lnb_ref[...]
    if emit_v:
        v_ref[...] = v
    vb = v.astype(MXU_DTYPE)
    for c in range(tm // CHUNK):
        r0, r1 = c * CHUNK, (c + 1) * CHUNK
        for g in range(A_GROUPS):
            sv = _dot(ws_ref[g], vb[r0:r1, g * dg:(g + 1) * dg]) + bs_ref[:, g:g + 1]
            gated[r0:r1, g * dg:(g + 1) * dg] = (u[r0:r1, g * dg:(g + 1) * dg] * sv).astype(MXU_DTYPE)
    o_ref[...] = x + _dot(gated[...], wout_ref[...])


def _gmlp_layer(h, g, w_in, ln_g, ln_b, ws_eff, bs_eff, w_out, *, emit_v, tm=256):
    m, d = h.shape
    tm = min(tm, m)
    d_a = ln_g.shape[-1]
    row = pl.BlockSpec((tm, d), lambda i: (i, 0))
    out_shape = [jax.ShapeDtypeStruct((m, d), F32)]
    out_specs = [row]
    if emit_v:
        out_shape.append(jax.ShapeDtypeStruct((m, d_a), F32))
        out_specs.append(pl.BlockSpec((tm, d_a), lambda i: (i, 0)))
    res = pl.pallas_call(
        functools.partial(_gmlp_kernel, emit_v=emit_v),
        grid=(m // tm,),
        in_specs=[row, _const_spec((1, d)), _const_spec(w_in.shape), _const_spec((1, d_a)),
                  _const_spec((1, d_a)), _const_spec(ws_eff.shape), _const_spec(bs_eff.shape),
                  _const_spec(w_out.shape)],
        out_specs=out_specs, out_shape=out_shape,
        scratch_shapes=[pltpu.VMEM((tm, d_a), MXU_DTYPE)],
        compiler_params=_cparams(1), name="gmlp_layer",
    )(h, g.reshape(1, d), w_in, ln_g.reshape(1, d_a), ln_b.reshape(1, d_a), ws_eff, bs_eff, w_out)
    return (res[0], res[1]) if emit_v else (res[0], None)


def _ffn_kernel(*refs, with_proj):
    if with_proj:
        h_ref, a_ref, wo_ref, g_ref, wg_ref, wu_ref, wd_ref, o_ref = refs
        x = h_ref[...] + _dot(a_ref[...], wo_ref[...])
    else:
        h_ref, g_ref, wg_ref, wu_ref, wd_ref, o_ref = refs
        x = h_ref[...]
    xn = _rms(x, g_ref[...]).astype(MXU_DTYPE)
    mid = (jax.nn.silu(_dot(xn, wg_ref[...])) * _dot(xn, wu_ref[...])).astype(MXU_DTYPE)
    o_ref[...] = x + _dot(mid, wd_ref[...])


def _ffn_layer(h, g, wg, wu, wd, attn=None, w_o=None, *, tm=256):
    m, d = h.shape
    tm = min(tm, m)
    row = pl.BlockSpec((tm, d), lambda i: (i, 0))
    with_proj = attn is not None
    args, specs = [h], [row]
    if with_proj:
        args += [attn, w_o]
        specs += [pl.BlockSpec((tm, attn.shape[1]), lambda i: (i, 0)), _const_spec(w_o.shape)]
    args += [g.reshape(1, d), wg, wu, wd]
    specs += [_const_spec((1, d)), _const_spec(wg.shape), _const_spec(wu.shape), _const_spec(wd.shape)]
    return pl.pallas_call(
        functools.partial(_ffn_kernel, with_proj=with_proj),
        grid=(m // tm,), in_specs=specs, out_specs=row,
        out_shape=jax.ShapeDtypeStruct((m, d), F32),
        compiler_params=_cparams(1), name="ffn_layer",
    )(*args)


def _kvproj_kernel(h_ref, g_ref, w_ref, gs_ref, gw_ref, cos_ref, sdn_ref, sup_ref,
                   cmp_ref, slc_ref, win_ref, slc16_ref, win16_ref):
    x = h_ref[...]
    xn = _rms(x, g_ref[...]).astype(MXU_DTYPE)
    kv = _dot(xn, w_ref[...])
    width = 2 * N_KV_HEADS * HEAD_DIM
    kw = N_KV_HEADS * HEAD_DIM
    cmp_ref[...] = kv[:, :width]
    cos, sdn, sup = cos_ref[...], sdn_ref[...], sup_ref[...]
    for base, gk_ref, o_ref, o16_ref in ((width, gs_ref, slc_ref, slc16_ref),
                                         (2 * width, gw_ref, win_ref, win16_ref)):
        for c in range(kw // LANES):
            k = _headnorm128(kv[:, base + c * LANES: base + (c + 1) * LANES], gk_ref[...])
            k = _rope128(k, cos, sdn, sup)
            o_ref[:, c * LANES:(c + 1) * LANES] = k
            o16_ref[:, c * LANES:(c + 1) * LANES] = k.astype(MXU_DTYPE)
        v = kv[:, base + kw: base + width]
        o_ref[:, kw:width] = v
        o16_ref[:, kw:width] = v.astype(MXU_DTYPE)


def _kv_project(h, kv_norm, kv_w, gk_slc, gk_win, tabs, *, tm=256):
    m, d = h.shape
    tm = min(tm, m)
    width = 2 * N_KV_HEADS * HEAD_DIM
    n_tab = tabs[0].shape[0] // tm
    row = pl.BlockSpec((tm, d), lambda i: (i, 0))
    tab = pl.BlockSpec((tm, LANES), lambda i: (i % n_tab, 0))
    orow = pl.BlockSpec((tm, width), lambda i: (i, 0))
    return pl.pallas_call(
        _kvproj_kernel, grid=(m // tm,),
        in_specs=[row, _const_spec((1, d)), _const_spec(kv_w.shape), _const_spec((1, LANES)),
                  _const_spec((1, LANES)), tab, tab, tab],
        out_specs=[orow] * 5,
        out_shape=[jax.ShapeDtypeStruct((m, width), F32)] * 3
                  + [jax.ShapeDtypeStruct((m, width), MXU_DTYPE)] * 2,
        compiler_params=_cparams(1), name="kv_project",
    )(h, kv_norm.reshape(1, d), kv_w, gk_slc, gk_win, *tabs)


def _qproj_kernel(h_ref, g_ref, wq_ref, wgt_ref, gq_ref, cos_ref, sdn_ref, sup_ref,
                  qn_ref, qr_ref, gates_ref):
    x = h_ref[...]
    xn = _rms(x, g_ref[...]).astype(MXU_DTYPE)
    q = _dot(xn, wq_ref[...])
    gates_ref[...] = jax.nn.sigmoid(_dot(xn, wgt_ref[...])).T
    cos, sdn, sup = cos_ref[...], sdn_ref[...], sup_ref[...]
    for c in range(q.shape[1] // LANES):
        qn = _headnorm128(q[:, c * LANES:(c + 1) * LANES], gq_ref[...])
        qn_ref[:, c * LANES:(c + 1) * LANES] = (qn * ATTN_SCALE).astype(qn_ref.dtype)
        qr_ref[:, c * LANES:(c + 1) * LANES] = (_rope128(qn, cos, sdn, sup) * ATTN_SCALE).astype(qr_ref.dtype)


def _q_project(h, g, w_q, w_gate, gq, tabs, *, tm=256):
    m, d = h.shape
    tm = min(tm, m)
    dq = w_q.shape[1]
    n_tab = tabs[0].shape[0] // tm
    row = pl.BlockSpec((tm, d), lambda i: (i, 0))
    tab = pl.BlockSpec((tm, LANES), lambda i: (i % n_tab, 0))
    qrow = pl.BlockSpec((tm, dq), lambda i: (i, 0))
    return pl.pallas_call(
        _qproj_kernel, grid=(m // tm,),
        in_specs=[row, _const_spec((1, d)), _const_spec(w_q.shape), _const_spec(w_gate.shape),
                  _const_spec((1, LANES)), tab, tab, tab],
        out_specs=[qrow, qrow, pl.BlockSpec((LANES, tm), lambda i: (0, i))],
        out_shape=[jax.ShapeDtypeStruct((m, dq), MXU_DTYPE)] * 2 + [jax.ShapeDtypeStruct((LANES, m), F32)],
        compiler_params=_cparams(1), name="q_project",
    )(h, g.reshape(1, d), w_q, w_gate, gq, *tabs)


def _compress_kernel(x_ref, pos_ref, w_ref, o_ref):
    width = 2 * N_KV_HEADS * HEAD_DIM
    kw = N_KV_HEADS * HEAD_DIM
    for kv in range(2):
        acc_a = jnp.zeros((x_ref.shape[0], kw), F32)
        acc_b = jnp.zeros((x_ref.shape[0], kw), F32)
        for l in range(CMP_STRIDE):
            z = x_ref[:, l * width + kv * kw: l * width + (kv + 1) * kw]
            pa = pos_ref[l:l + 1, kv * kw:(kv + 1) * kw]
            pb = pos_ref[CMP_STRIDE + l:CMP_STRIDE + l + 1, kv * kw:(kv + 1) * kw]
            acc_a = acc_a + _dot((z + pa).astype(MXU_DTYPE), w_ref[kv, l])
            acc_b = acc_b + _dot((z + pb).astype(MXU_DTYPE), w_ref[kv, CMP_STRIDE + l])
        o_ref[:, kv * kw:(kv + 1) * kw] = acc_a
        o_ref[:, width + kv * kw: width + (kv + 1) * kw] = acc_b


def _compress_partial(rows, pos_tab, w_bd, *, mb=256):
    n_sub, flat = rows.shape
    mb = min(mb, n_sub)
    width = 2 * N_KV_HEADS * HEAD_DIM
    return pl.pallas_call(
        _compress_kernel, grid=(n_sub // mb,),
        in_specs=[pl.BlockSpec((mb, flat), lambda i: (i, 0)), _const_spec(pos_tab.shape),
                  _const_spec(w_bd.shape)],
        out_specs=pl.BlockSpec((mb, 2 * width), lambda i: (i, 0)),
        out_shape=jax.ShapeDtypeStruct((n_sub, 2 * width), F32),
        compiler_params=_cparams(1), name="compress_partial",
    )(rows, pos_tab, w_bd)


def _assemble_kc_vc(ab, gk):
    width = 2 * N_KV_HEADS * HEAD_DIM
    kw = N_KV_HEADS * HEAD_DIM
    n = ab.shape[0]
    kcv = ab[:, :width] + pltpu.roll(ab[:, width:], n - 1, 0)
    kc = jnp.concatenate([_headnorm128(kcv[:, c * LANES:(c + 1) * LANES], gk) for c in range(kw // LANES)],
                         axis=1)
    return kc, kcv[:, kw:width]


def _kcvc_kernel(ab_ref, gk_ref, kc_ref, vc_ref):
    kc, vc = _assemble_kc_vc(ab_ref[0], gk_ref[...])
    kc_ref[0] = kc.astype(kc_ref.dtype)
    vc_ref[0] = vc.astype(vc_ref.dtype)


def _prompt_kc_vc(ab, gk):
    b, n, w = ab.shape
    kw = N_KV_HEADS * HEAD_DIM
    spec = pl.BlockSpec((1, n, kw), lambda i: (i, 0, 0))
    return pl.pallas_call(
        _kcvc_kernel, grid=(b,),
        in_specs=[pl.BlockSpec((1, n, w), lambda i: (i, 0, 0)), _const_spec((1, LANES))],
        out_specs=[spec, spec], out_shape=[jax.ShapeDtypeStruct((b, n, kw), MXU_DTYPE)] * 2,
        compiler_params=_cparams(1), name="prompt_kc_vc",
    )(ab, gk)


def _select_blocks(imp, blk, qpos, axis, n_sel):
    cur = qpos // SEL_BLOCK
    valid = blk <= cur
    forced = (blk == 0) | (blk == cur) | (blk == cur - 1)
    sc = jnp.where(forced, jnp.inf, jnp.where(valid, imp, -jnp.inf))
    cnt = jnp.zeros(sc.shape, jnp.int32)
    for j in range(n_sel):
        sj = lax.slice_in_dim(sc, j, j + 1, axis=axis)
        beats = (sj > sc) | ((sj == sc) & (blk > j))
        cnt = cnt + beats.astype(jnp.int32)
    return (cnt < min(SEL_TOPK, n_sel)) & (sc > -jnp.inf) & (blk < n_sel)


def _dot_tn(a, b):
    return lax.dot_general(a, b, (((0,), (0,)), ((), ())), preferred_element_type=F32)


def _attn_prompt_kernel(qn_ref, qr_ref, gt_ref, kc_ref, vc_ref, slc_ref, win_ref, et_ref, mt_ref, o_ref,
                        bias_s, m_s, l_s, acc_s, *, n_sel, tk):
    tq = qn_ref.shape[0]
    t_len = slc_ref.shape[1]
    kw = N_KV_HEADS * HEAD_DIM
    rpg = 4
    ns = 2 * rpg
    qb = pl.program_id(1)
    lo = lax.broadcasted_iota(jnp.int32, (tq, LANES), 1) < HEAD_DIM
    top = lax.broadcasted_iota(jnp.int32, (LANES, tq), 0) < HEAD_DIM
    wlen = min(WINDOW + tq, t_len)

    def q_of(shape):
        return qb * tq + lax.broadcasted_iota(jnp.int32, shape, 1)

    def stack(ref, p):
        parts = []
        for side in range(2):
            keep = lo if side == 0 else jnp.logical_not(lo)
            for r in range(rpg):
                c = p * rpg + r
                t = ref[:, c * LANES:(c + 1) * LANES]
                parts.append(jnp.where(keep, t, jnp.zeros_like(t)))
        return jnp.concatenate(parts, axis=0)

    def slabs(x0, x1):
        return jnp.concatenate([x0] * rpg + [x1] * rpg, axis=1)

    for p in range(2):
        qn_st = stack(qn_ref, p)
        s = _dot_nt(kc_ref[0, :, p * LANES:(p + 1) * LANES], qn_st)
        n_cb = s.shape[0]
        cm1 = (lax.broadcasted_iota(jnp.int32, (n_cb, tq), 0) * CMP_STRIDE + (CMP_LEN - 1)) <= q_of((n_cb, tq))
        cmask = slabs(cm1, cm1)
        s = jnp.where(cmask, s, -jnp.inf)
        mx = jnp.max(s, axis=0, keepdims=True)
        mx = jnp.where(mx == -jnp.inf, 0.0, mx)
        e = jnp.where(cmask, jnp.exp(s - mx), 0.0)
        pr = e / jnp.maximum(jnp.sum(e, axis=0, keepdims=True), 1e-30)
        o_cmp = _dot_tn(vc_ref[0, :, p * LANES:(p + 1) * LANES], pr.astype(MXU_DTYPE))

        for side in range(2):
            g = 2 * p + side
            imp = pr[:, side * rpg * tq:(side * rpg + 1) * tq]
            for r in range(1, rpg):
                imp = imp + pr[:, (side * rpg + r) * tq:(side * rpg + r + 1) * tq]
            imp_t = _dot_exact_lhs(mt_ref[...], imp)[:n_sel]
            blk = lax.broadcasted_iota(jnp.int32, (n_sel, tq), 0)
            sel = _select_blocks(imp_t, blk, q_of((n_sel, tq)), 0, n_sel)
            sel_f = jnp.concatenate([jnp.where(sel, 1.0, 0.0).astype(F32),
                                     jnp.zeros((LANES - n_sel, tq), F32)], axis=0)
            al = _dot(et_ref[...], sel_f.astype(MXU_DTYPE))
            kpos = lax.broadcasted_iota(jnp.int32, (t_len, tq), 0)
            bias_s[g] = jnp.where((al > 0.5) & (kpos <= q_of((t_len, tq))), 0.0, NEG)

        qr_st = stack(qr_ref, p)
        m_s[...] = jnp.full(m_s.shape, NEG, F32)
        l_s[...] = jnp.zeros(l_s.shape, F32)
        acc_s[...] = jnp.zeros(acc_s.shape, F32)

        def body(kt, carry, p=p, qr_st=qr_st):
            k0 = pl.multiple_of(kt * tk, tk)
            k = slc_ref[0, pl.ds(k0, tk), p * LANES:(p + 1) * LANES]
            v = slc_ref[0, pl.ds(k0, tk), kw + p * LANES: kw + (p + 1) * LANES]
            s = _dot_nt(k, qr_st) + slabs(bias_s[2 * p, pl.ds(k0, tk), :], bias_s[2 * p + 1, pl.ds(k0, tk), :])
            m_prev = m_s[...]
            m_new = jnp.maximum(m_prev, jnp.max(s, axis=0, keepdims=True))
            alpha = jnp.exp(m_prev - m_new)
            pe = jnp.exp(s - m_new)
            l_s[...] = alpha * l_s[...] + jnp.sum(pe, axis=0, keepdims=True)
            acc_s[...] = alpha * acc_s[...] + _dot_tn(v, pe.astype(MXU_DTYPE))
            m_s[...] = m_new
            return carry

        lax.fori_loop(0, (qb * tq + tq - 1) // tk + 1, body, 0)
        o_slc = acc_s[...] / jnp.maximum(l_s[...], 1e-30)

        k0 = pl.multiple_of(jnp.maximum(qb * tq + tq - wlen, 0), tq)
        k = win_ref[0, pl.ds(k0, wlen), p * LANES:(p + 1) * LANES]
        v = win_ref[0, pl.ds(k0, wlen), kw + p * LANES: kw + (p + 1) * LANES]
        kpos = k0 + lax.broadcasted_iota(jnp.int32, (wlen, tq), 0)
        qp = q_of((wlen, tq))
        wb = jnp.where((kpos <= qp) & (kpos > qp - WINDOW), 0.0, NEG)
        s = _dot_nt(k, qr_st) + slabs(wb, wb)
        pe = jnp.exp(s - jnp.max(s, axis=0, keepdims=True))
        o_win = _dot_tn(v, pe.astype(MXU_DTYPE)) / jnp.maximum(jnp.sum(pe, axis=0, keepdims=True), 1e-30)

        gt = gt_ref[...]
        for r in range(rpg):
            c = p * rpg + r
            h_lo = (2 * p) * rpg + r
            h_hi = (2 * p + 1) * rpg + r

            def gate(br):
                return jnp.where(top, gt[h_lo * N_BRANCH + br: h_lo * N_BRANCH + br + 1, :],
                                 gt[h_hi * N_BRANCH + br: h_hi * N_BRANCH + br + 1, :])

            def pick(x):
                return jnp.where(top, x[:, r * tq:(r + 1) * tq], x[:, (rpg + r) * tq:(rpg + r + 1) * tq])

            o = gate(0) * pick(o_cmp) + gate(1) * pick(o_slc) + gate(2) * pick(o_win)
            o_ref[:, c * LANES:(c + 1) * LANES] = o.T.astype(o_ref.dtype)


def _attn_prompt(qn, qr, gates_t, kc, vc, slc16, win16, e_t, mt, *, n_sel, tq=128, tk=512):
    b, t, w = slc16.shape
    nq = t // tq
    tk = min(tk, t)
    dq = qn.shape[1]
    n_cb = kc.shape[1]
    assert tq == LANES and t % tk == 0 and tk % tq == 0
    qspec = pl.BlockSpec((tq, dq), lambda i, j: (i * nq + j, 0))
    seq = lambda shape: pl.BlockSpec(shape, lambda i, j: (i, 0, 0))
    return pl.pallas_call(
        functools.partial(_attn_prompt_kernel, n_sel=n_sel, tk=tk),
        grid=(b, nq),
        in_specs=[qspec, qspec, pl.BlockSpec((LANES, tq), lambda i, j: (0, i * nq + j)),
                  seq((1, n_cb, kc.shape[2])), seq((1, n_cb, vc.shape[2])),
                  seq((1, t, w)), seq((1, t, w)), _const_spec(e_t.shape), _const_spec(mt.shape)],
        out_specs=qspec, out_shape=jax.ShapeDtypeStruct((b * t, dq), MXU_DTYPE),
        scratch_shapes=[pltpu.VMEM((N_KV_HEADS, t, tq), F32),
                        pltpu.VMEM((1, 8 * tq), F32), pltpu.VMEM((1, 8 * tq), F32),
                        pltpu.VMEM((LANES, 8 * tq), F32)],
        compiler_params=_cparams(2), name="attn_prompt",
    )(qn, qr, gates_t, kc, vc, slc16, win16, e_t, mt)


def _attn_sample_kernel(pt_ref, qn_ref, qr_ref, gt_ref, gk_ref, rsum_ref, m_ref, e_ref, abn_ref, sn_ref,
                        wst_ref, wn_ref, *rest, n_pages, n_sel, past_len, t_new):
    ab_pages = rest[:n_pages]
    slc_pages = rest[n_pages:2 * n_pages]
    o_ref, wout_ref, pad_s = rest[2 * n_pages:]
    kw = N_KV_HEADS * HEAD_DIM
    page = slc_pages[0].shape[1]
    nrow = qn_ref.shape[1]
    rows_per_g = nrow // N_KV_HEADS
    qn = qn_ref[0]
    qr = qr_ref[0]
    t_of_row = lax.broadcasted_iota(jnp.int32, (nrow, 1), 0) % t_new
    qpos = past_len + t_of_row

    def softmax(s, mask):
        s = jnp.where(mask, s, -jnp.inf)
        mx = jnp.max(s, axis=-1, keepdims=True)
        mx = jnp.where(mx == -jnp.inf, 0.0, mx)
        e = jnp.where(mask, jnp.exp(s - mx), 0.0)
        return e / jnp.maximum(jnp.sum(e, axis=-1, keepdims=True), 1e-30)

    n_ab = m_ref.shape[0]
    pieces = [r[0] for r in ab_pages] + [abn_ref[0]]
    n_have = sum(x.shape[0] for x in pieces)
    pieces.append(jnp.zeros((n_ab - n_have, pieces[0].shape[1]), F32))
    kc, vc = _assemble_kc_vc(jnp.concatenate(pieces, axis=0), gk_ref[...])
    blk_c = lax.broadcasted_iota(jnp.int32, (nrow, n_ab), 1)
    cmask = (blk_c * CMP_STRIDE + (CMP_LEN - 1)) <= qpos
    pc = softmax(_dot_nt(qn, kc.astype(MXU_DTYPE)), cmask)
    o_cmp = _dot(pc.astype(MXU_DTYPE), vc.astype(MXU_DTYPE))

    imp = _dot_exact_lhs(rsum_ref[...], pc)
    imp_s = _dot_exact_rhs(imp, m_ref[...])
    blk = lax.broadcasted_iota(jnp.int32, imp_s.shape, 1)
    sel = _select_blocks(imp_s, blk, jnp.broadcast_to(qpos, imp_s.shape), 1, n_sel)
    al = _dot(jnp.where(sel, 1.0, 0.0).astype(MXU_DTYPE), e_ref[...])
    n_keys = al.shape[1]
    kpos = lax.broadcasted_iota(jnp.int32, (nrow, n_keys), 1)
    smask = (al > 0.5) & (kpos <= qpos)

    pad_s[...] = jnp.zeros(pad_s.shape, F32)
    pad_s[0:sn_ref.shape[1], :] = sn_ref[0]
    tiles = [r[0] for r in slc_pages] + [pad_s[...]]
    s = jnp.concatenate([_dot_nt(qr, t[:, :kw].astype(MXU_DTYPE)) for t in tiles], axis=1)
    ps = softmax(s, smask).astype(MXU_DTYPE)
    o_slc = jnp.zeros((nrow, kw), F32)
    for i, t in enumerate(tiles):
        o_slc = o_slc + _dot(ps[:, i * page:(i + 1) * page], t[:, kw:].astype(MXU_DTYPE))

    wbuf = wst_ref.shape[1]
    pad_s[...] = jnp.zeros(pad_s.shape, F32)
    pad_s[0:wn_ref.shape[1], :] = wn_ref[0]
    wst = wst_ref[0]
    wtiles = [wst[i * page:(i + 1) * page] for i in range(wbuf // page)] + [pad_s[...]]
    s = jnp.concatenate([_dot_nt(qr, t[:, :kw].astype(MXU_DTYPE)) for t in wtiles], axis=1)
    wpos = (past_len - wbuf) + lax.broadcasted_iota(jnp.int32, s.shape, 1)
    wmask = (wpos <= qpos) & (wpos > qpos - WINDOW) & (wpos >= 0)
    pw = softmax(s, wmask).astype(MXU_DTYPE)
    o_win = jnp.zeros((nrow, kw), F32)
    for i, t in enumerate(wtiles):
        o_win = o_win + _dot(pw[:, i * page:(i + 1) * page], t[:, kw:].astype(MXU_DTYPE))

    gt = gt_ref[0]
    o = gt[:, 0:1] * o_cmp + gt[:, 1:2] * o_slc + gt[:, 2:3] * o_win
    lane_g = lax.broadcasted_iota(jnp.int32, (rows_per_g, kw), 1) // HEAD_DIM
    acc = jnp.zeros((rows_per_g, kw), F32)
    for g in range(N_KV_HEADS):
        acc = acc + jnp.where(lane_g == g, o[g * rows_per_g:(g + 1) * rows_per_g], 0.0)
    o_ref[0] = acc

    rolled = pltpu.roll(wst, wbuf - t_new, 0)
    wout_ref[0, 0:wbuf - 8, :] = rolled[0:wbuf - 8]
    tail_new = pltpu.roll(wn_ref[0], 8 - t_new, 0)
    r8 = lax.broadcasted_iota(jnp.int32, tail_new.shape, 0)
    wout_ref[0, wbuf - 8:wbuf, :] = jnp.where(r8 >= 8 - t_new, tail_new, rolled[wbuf - 8:wbuf])


def _attn_sample(page_table, qn, qr, gates, gk, rsum, m_mat, e_mat, ab_pool, ab_new, cache_slc, slc_new,
                 win_state, win_new, *, n_sel, past_len, t_new):
    nb, n_pages = page_table.shape
    nrow, kw = qn.shape[1], qn.shape[2]
    page = cache_slc.shape[1]
    width = cache_slc.shape[2]
    wbuf = win_state.shape[1]
    per_b = lambda shape: pl.BlockSpec(shape, lambda i, pt: (i, 0, 0))

    def page_spec(shape, p):
        return pl.BlockSpec(shape, lambda i, pt: (pt[i * n_pages + p], 0, 0))

    in_specs = [per_b((1, nrow, kw)), per_b((1, nrow, kw)), per_b((1, nrow, N_BRANCH)),
                pl.BlockSpec((1, LANES), lambda i, pt: (0, 0)),
                pl.BlockSpec(rsum.shape, lambda i, pt: (0, 0)),
                pl.BlockSpec(m_mat.shape, lambda i, pt: (0, 0)),
                pl.BlockSpec(e_mat.shape, lambda i, pt: (0, 0)),
                per_b((1,) + ab_new.shape[1:]), per_b((1,) + slc_new.shape[1:]),
                per_b((1, wbuf, width)), per_b((1,) + win_new.shape[1:])]
    in_specs += [page_spec((1,) + ab_pool.shape[1:], p) for p in range(n_pages)]
    in_specs += [page_spec((1, page, width), p) for p in range(n_pages)]
    grid_spec = pltpu.PrefetchScalarGridSpec(
        num_scalar_prefetch=1, grid=(nb,), in_specs=in_specs,
        out_specs=[per_b((1, nrow // N_KV_HEADS, kw)), per_b((1, wbuf, width))],
        scratch_shapes=[pltpu.VMEM((page, width), F32)])
    return pl.pallas_call(
        functools.partial(_attn_sample_kernel, n_pages=n_pages, n_sel=n_sel, past_len=past_len, t_new=t_new),
        grid_spec=grid_spec,
        out_shape=[jax.ShapeDtypeStruct((nb, nrow // N_KV_HEADS, kw), F32),
                   jax.ShapeDtypeStruct((nb, wbuf, width), F32)],
        compiler_params=_cparams(1), name="attn_sample",
    )(page_table.reshape(-1), qn, qr, gates, gk, rsum, m_mat, e_mat, ab_new, slc_new, win_state, win_new,
      *([ab_pool] * n_pages), *([cache_slc] * n_pages))


def _rope_tables(pos):
    half = ROPE_DIM // 2
    inv_freq = ROPE_THETA ** (-jnp.arange(half, dtype=F32) / half)
    ang = pos.astype(F32)[:, None] * inv_freq[None, :]
    cos, sin = jnp.cos(ang), jnp.sin(ang)
    n = pos.shape[0]
    z = jnp.zeros((n, HEAD_DIM - ROPE_DIM), F32)
    zh = jnp.zeros((n, half), F32)
    c_head = jnp.concatenate([cos, cos, z + 1.0], axis=1)
    dn_head = jnp.concatenate([zh, sin, z], axis=1)
    up_head = jnp.concatenate([-sin, zh, z], axis=1)
    tile2 = lambda a: jnp.concatenate([a, a], axis=1)
    return tile2(c_head), tile2(dn_head), tile2(up_head)


def _sel_matrices(n_cmp, n_cmp_pad, n_sel, n_keys):
    i = jnp.arange(n_cmp_pad)[:, None]
    j = jnp.arange(LANES)[None, :]
    m = ((i * CMP_STRIDE < j * SEL_BLOCK + SEL_BLOCK) & (i * CMP_STRIDE + CMP_LEN > j * SEL_BLOCK)
         & (i < n_cmp) & (j < n_sel))
    e = (jnp.arange(n_keys)[None, :] // SEL_BLOCK == jnp.arange(LANES)[:, None]) & (jnp.arange(LANES)[:, None] < n_sel)
    return m.astype(MXU_DTYPE), e.astype(MXU_DTYPE)


def _tile2(g):
    return jnp.concatenate([g, g]).reshape(1, LANES).astype(F32)


def kernel(x_prompt, x_sample, cache_kv_cmp, cache_kv_slc, state_kv_win, page_table, norm_mix, norm_ffn,
           ffn_w_gate, ffn_w_up, ffn_w_down, a_w_in, a_ln_g, a_ln_b, a_w_spatial, a_b_spatial, a_w_out,
           kv_norm, kv_w, k_norm_cmp, k_norm_slc, k_norm_win, cmp_pos_k, cmp_pos_v, cmp_w_k, cmp_w_v,
           b_w_in, b_q_norm, b_w_out):
    bf = MXU_DTYPE
    depth = norm_mix.shape[0]
    n_a = a_w_in.shape[0]
    bsz, t_p, d_model = x_prompt.shape
    nb, t_s, _ = x_sample.shape
    n_pages = page_table.shape[1]
    page = cache_kv_cmp.shape[1]
    past_len = n_pages * page
    width = 2 * N_KV_HEADS * HEAD_DIM
    kw = N_KV_HEADS * HEAD_DIM
    n_heads = b_w_out.shape[1] // HEAD_DIM
    rpg = n_heads // N_KV_HEADS
    dq = n_heads * HEAD_DIM
    assert N_KV_HEADS == 4 and rpg == 4 and t_p % CHUNK == 0 and CHUNK % t_s == 0 and t_s <= 8

    wg, wu, wd = ffn_w_gate.astype(bf), ffn_w_up.astype(bf), ffn_w_down.astype(bf)
    a_in, a_out = a_w_in.astype(bf), a_w_out.astype(bf)
    tril = jnp.tril(jnp.ones((CHUNK, CHUNK), bool))
    ws_p = jnp.where(tril[None, None], a_w_spatial, 0.0).astype(bf)
    bs_p = jnp.swapaxes(a_b_spatial, 1, 2)
    rep = CHUNK // t_s
    ws4 = jnp.where(tril[None, None, :t_s, :t_s], a_w_spatial[:, :, :t_s, :t_s], 0.0)
    ws_s = jnp.einsum('ab,lgts->lgatbs', jnp.eye(rep, dtype=F32), ws4).reshape(n_a, A_GROUPS, CHUNK, CHUNK).astype(bf)
    bs_s = jnp.tile(jnp.swapaxes(a_b_spatial[:, :, :t_s], 1, 2), (1, rep, 1))
    perm = jnp.array([(2 * p + side) * rpg + r for p in range(2) for r in range(rpg) for side in range(2)])
    wq = b_w_in[:, :, :dq].reshape(-1, d_model, n_heads, HEAD_DIM)[:, :, perm].reshape(-1, d_model, dq).astype(bf)
    wgate = jnp.pad(b_w_in[:, :, dq:], ((0, 0), (0, 0), (0, LANES - N_BRANCH * n_heads))).astype(bf)
    wo_pair = b_w_out.reshape(-1, n_heads, HEAD_DIM, d_model)[:, perm].reshape(-1, dq, d_model).astype(bf)
    wo_flat = b_w_out.astype(bf)
    kvw = kv_w.astype(bf)
    eye_g = jnp.eye(N_KV_HEADS, dtype=F32)
    w_bd = jnp.stack([jnp.einsum('ab,lde->ladbe', eye_g, w).reshape(CMP_LEN, kw, kw)
                      for w in (cmp_w_k, cmp_w_v)]).astype(bf)
    pos_tab = jnp.concatenate([jnp.tile(cmp_pos_k, (1, N_KV_HEADS)), jnp.tile(cmp_pos_v, (1, N_KV_HEADS))], axis=1)
    gk_cmp, gk_slc, gk_win = _tile2(k_norm_cmp), _tile2(k_norm_slc), _tile2(k_norm_win)

    def dense_a_layers(h, ws_eff, bs_eff, emit_v):
        vs = []
        for l in range(n_a):
            h, v = _gmlp_layer(h, norm_mix[l], a_in[l], a_ln_g[l], a_ln_b[l], ws_eff[l], bs_eff[l], a_out[l],
                               emit_v=emit_v)
            vs.append(v)
            h = _ffn_layer(h, norm_ffn[l], wg[l], wu[l], wd[l])
        return h, vs

    tabs_p = _rope_tables(jnp.arange(t_p))
    h = x_prompt.reshape(bsz * t_p, d_model)
    h, _ = dense_a_layers(h, ws_p, bs_p, False)
    kv_cmp_p, kv_slc_p, kv_win_p, slc16, win16 = _kv_project(h, kv_norm, kvw, gk_slc, gk_win, tabs_p)
    ab = _compress_partial(kv_cmp_p.reshape(bsz * t_p // CMP_STRIDE, CMP_STRIDE * width), pos_tab, w_bd)
    n_sub = t_p // CMP_STRIDE
    kc, vc = _prompt_kc_vc(ab.reshape(bsz, n_sub, 2 * width), gk_cmp)
    n_sel_p = t_p // SEL_BLOCK
    m_p, e_p = _sel_matrices(n_sub - 1, n_sub, n_sel_p, t_p)
    for j in range(depth - n_a):
        l = n_a + j
        qn, qr, gates = _q_project(h, norm_mix[l], wq[j], wgate[j], _tile2(b_q_norm[j]), tabs_p)
        o = _attn_prompt(qn, qr, gates, kc, vc, slc16.reshape(bsz, t_p, width), win16.reshape(bsz, t_p, width),
                         e_p.T, m_p.T, n_sel=n_sel_p)
        h = _ffn_layer(h, norm_ffn[l], wg[l], wu[l], wd[l], attn=o, w_o=wo_pair[j])
    y_prompt = h.reshape(bsz, t_p, d_model)
    row5 = lambda a, b_, t_: a.reshape(b_, t_, 2, N_KV_HEADS, HEAD_DIM)
    w_keep = min(WINDOW, t_p)
    kv_win_prompt = row5(kv_win_p, bsz, t_p)[:, t_p - w_keep:]

    m_s = nb * t_s
    tabs_s = _rope_tables(past_len + (jnp.arange(nb * t_s) % t_s))
    hs = x_sample.reshape(m_s, d_model)
    hs, vs = dense_a_layers(hs, ws_s, bs_s, True)
    a_v_sample = jnp.stack(vs).reshape(n_a, nb, t_s, -1)
    kv_cmp_s, kv_slc_s, kv_win_s, _, _ = _kv_project(hs, kv_norm, kvw, gk_slc, gk_win, tabs_s)
    n_pool = cache_kv_cmp.shape[0]
    sub_pp = page // CMP_STRIDE
    ab_pool = _compress_partial(cache_kv_cmp.reshape(n_pool * sub_pp, CMP_STRIDE * width), pos_tab, w_bd)
    ab_pool = ab_pool.reshape(n_pool, sub_pp, 2 * width)
    t_full = -(-(past_len + t_s) // SEL_BLOCK) * SEL_BLOCK
    n_new_sub = (t_full - past_len) // CMP_STRIDE
    new_cmp = jnp.pad(kv_cmp_s.reshape(nb, t_s, width), ((0, 0), (0, t_full - past_len - t_s), (0, 0)))
    ab_new = _compress_partial(new_cmp.reshape(nb * n_new_sub, CMP_STRIDE * width), pos_tab, w_bd)
    ab_new = jnp.pad(ab_new.reshape(nb, n_new_sub, 2 * width), ((0, 0), (0, 8 - n_new_sub), (0, 0)))
    n_cmp_s = t_full // CMP_STRIDE - 1
    n_sel_s = t_full // SEL_BLOCK
    n_cmp_pad = 256
    n_keys = past_len + page
    m_sm, e_sm = _sel_matrices(n_cmp_s, n_cmp_pad, n_sel_s, n_keys)
    nrow = n_heads * t_s
    ridx = jnp.arange(nrow)
    rsum = ((ridx[:, None] // (rpg * t_s) == ridx[None, :] // (rpg * t_s))
            & (ridx[:, None] % t_s == ridx[None, :] % t_s)).astype(bf)
    pad8 = lambda a: jnp.pad(a.reshape(nb, t_s, width), ((0, 0), (0, 8 - t_s), (0, 0)))
    slc_new, win_new = pad8(kv_slc_s), pad8(kv_win_s)
    inv_perm = jnp.argsort(perm)

    def pad_q(q):
        q = q.reshape(nb, t_s, n_heads, HEAD_DIM)[:, :, inv_perm].reshape(nb, t_s, N_KV_HEADS, rpg, HEAD_DIM)
        q = jnp.transpose(q, (0, 2, 3, 1, 4))
        q = jnp.einsum('bgrtd,gk->bgrtkd', q, jnp.eye(N_KV_HEADS, dtype=q.dtype))
        return q.reshape(nb, nrow, kw)

    kv_win_sample = None
    for j in range(depth - n_a):
        l = n_a + j
        qn, qr, gates = _q_project(hs, norm_mix[l], wq[j], wgate[j], _tile2(b_q_norm[j]), tabs_s)
        g3 = gates[:N_BRANCH * n_heads].T.reshape(nb, t_s, N_KV_HEADS, rpg, N_BRANCH)
        g3 = jnp.transpose(g3, (0, 2, 3, 1, 4)).reshape(nb, nrow, N_BRANCH)
        o, kv_win_sample = _attn_sample(page_table, pad_q(qn), pad_q(qr), g3, gk_cmp, rsum, m_sm, e_sm,
                                        ab_pool, ab_new, cache_kv_slc.reshape(n_pool, page, width), slc_new,
                                        state_kv_win.reshape(nb, -1, width), win_new,
                                        n_sel=n_sel_s, past_len=past_len, t_new=t_s)
        o = jnp.transpose(o.reshape(nb, rpg, t_s, N_KV_HEADS, HEAD_DIM), (0, 2, 3, 1, 4)).reshape(m_s, dq)
        hs = _ffn_layer(hs, norm_ffn[l], wg[l], wu[l], wd[l], attn=o.astype(bf), w_o=wo_flat[j])
    y_sample = hs.reshape(nb, t_s, d_model)

    return (y_prompt, y_sample, row5(kv_cmp_p, bsz, t_p), row5(kv_cmp_s, nb, t_s),
            row5(kv_slc_p, bsz, t_p), row5(kv_slc_s, nb, t_s), kv_win_prompt,
            kv_win_sample.reshape(state_kv_win.shape), a_v_sample)
```

```python
import functools

import jax
import jax.numpy as jnp
from jax import lax
from jax.experimental import pallas as pl
from jax.experimental.pallas import tpu as pltpu

CHUNK = 128
A_GROUPS = 8
HEAD_DIM = 64
N_KV_HEADS = 4
ROPE_DIM = HEAD_DIM // 4
ROPE_THETA = 500000.0
CMP_STRIDE = 16
CMP_LEN = 2 * CMP_STRIDE
SEL_BLOCK = 64
SEL_TOPK = 16
WINDOW = 512
N_BRANCH = 3
EPS = 1e-6
ATTN_SCALE = HEAD_DIM ** -0.5

LANES = 128
VMEM_LIMIT_BYTES = 56 * 1024 * 1024
MXU_DTYPE = jnp.bfloat16
NEG = -1e30
F32 = jnp.float32


def _cparams(n_grid):
    return pltpu.CompilerParams(dimension_semantics=("arbitrary",) * n_grid,
                                vmem_limit_bytes=VMEM_LIMIT_BYTES)


def _const_spec(shape):
    nd = len(shape)
    return pl.BlockSpec(shape, lambda *_: (0,) * nd)


def _dot(a, b):
    return jnp.dot(a, b, preferred_element_type=F32)


def _dot_nt(a, b):
    return lax.dot_general(a, b, (((1,), (1,)), ((), ())), preferred_element_type=F32)


def _split3(x):
    hi = x.astype(MXU_DTYPE)
    r1 = x - hi.astype(F32)
    mid = r1.astype(MXU_DTYPE)
    lo = (r1 - mid.astype(F32)).astype(MXU_DTYPE)
    return hi, mid, lo


def _dot_exact_lhs(a, x):
    hi, mid, lo = _split3(x)
    return _dot(a, hi) + _dot(a, mid) + _dot(a, lo)


def _dot_exact_rhs(x, b):
    hi, mid, lo = _split3(x)
    return _dot(hi, b) + _dot(mid, b) + _dot(lo, b)


def _rms(x, g):
    return x * lax.rsqrt(jnp.mean(x * x, axis=-1, keepdims=True) + EPS) * g


def _headnorm128(x, g):
    lo = lax.broadcasted_iota(jnp.int32, x.shape, 1) < HEAD_DIM
    sq = x * x
    s_lo = jnp.sum(jnp.where(lo, sq, 0.0), axis=-1, keepdims=True)
    s_hi = jnp.sum(jnp.where(lo, 0.0, sq), axis=-1, keepdims=True)
    ms = jnp.where(lo, s_lo, s_hi) * (1.0 / HEAD_DIM)
    return x * lax.rsqrt(ms + EPS) * g


def _rope128(x, cos, s_dn, s_up):
    half = ROPE_DIM // 2
    return x * cos + pltpu.roll(x, half, 1) * s_dn + pltpu.roll(x, LANES - half, 1) * s_up


def _gmlp_kernel(h_ref, g_ref, win_ref, lng_ref, lnb_ref, ws_ref, bs_ref, wout_ref, *rest, emit_v):
    if emit_v:
        o_ref, v_ref, gated = rest
    else:
        o_ref, gated = rest
    tm = h_ref.shape[0]
    d_a = lng_ref.shape[1]
    dg = d_a // A_GROUPS
    x = h_ref[...]
    xn = _rms(x, g_ref[...]).astype(MXU_DTYPE)
    u = jax.nn.gelu(_dot(xn, win_ref[:, :d_a]))
    v = jax.nn.gelu(_dot(xn, win_ref[:, d_a:]))
    vc = v - jnp.mean(v, axis=-1, keepdims=True)
    v = vc * lax.rsqrt(jnp.mean(vc * vc, axis=-1, keepdims=True) + EPS) * lng_ref[...] + lnb_ref[...]
    if emit_v:
        v_ref[...] = v
    vb = v.astype(MXU_DTYPE)
    for c in range(tm // CHUNK):
        r0, r1 = c * CHUNK, (c + 1) * CHUNK
        for g in range(A_GROUPS):
            sv = _dot(ws_ref[g], vb[r0:r1, g * dg:(g + 1) * dg]) + bs_ref[:, g:g + 1]
            gated[r0:r1, g * dg:(g + 1) * dg] = (u[r0:r1, g * dg:(g + 1) * dg] * sv).astype(MXU_DTYPE)
    o_ref[...] = x + _dot(gated[...], wout_ref[...])


def _gmlp_layer(h, g, w_in, ln_g, ln_b, ws_eff, bs_eff, w_out, *, emit_v, tm=256):
    m, d = h.shape
    tm = min(tm, m)
    d_a = ln_g.shape[-1]
    row = pl.BlockSpec((tm, d), lambda i: (i, 0))
    out_shape = [jax.ShapeDtypeStruct((m, d), F32)]
    out_specs = [row]
    if emit_v:
        out_shape.append(jax.ShapeDtypeStruct((m, d_a), F32))
        out_specs.append(pl.BlockSpec((tm, d_a), lambda i: (i, 0)))
    res = pl.pallas_call(
        functools.partial(_gmlp_kernel, emit_v=emit_v),
        grid=(m // tm,),
        in_specs=[row, _const_spec((1, d)), _const_spec(w_in.shape), _const_spec((1, d_a)),
                  _const_spec((1, d_a)), _const_spec(ws_eff.shape), _const_spec(bs_eff.shape),
                  _const_spec(w_out.shape)],
        out_specs=out_specs, out_shape=out_shape,
        scratch_shapes=[pltpu.VMEM((tm, d_a), MXU_DTYPE)],
        compiler_params=_cparams(1), name="gmlp_layer",
    )(h, g.reshape(1, d), w_in, ln_g.reshape(1, d_a), ln_b.reshape(1, d_a), ws_eff, bs_eff, w_out)
    return (res[0], res[1]) if emit_v else (res[0], None)


def _ffn_kernel(*refs, with_proj):
    if with_proj:
        h_ref, a_ref, wo_ref, g_ref, wg_ref, wu_ref, wd_ref, o_ref = refs
        x = h_ref[...] + _dot(a_ref[...], wo_ref[...])
    else:
        h_ref, g_ref, wg_ref, wu_ref, wd_ref, o_ref = refs
        x = h_ref[...]
    xn = _rms(x, g_ref[...]).astype(MXU_DTYPE)
    mid = (jax.nn.silu(_dot(xn, wg_ref[...])) * _dot(xn, wu_ref[...])).astype(MXU_DTYPE)
    o_ref[...] = x + _dot(mid, wd_ref[...])


def _ffn_layer(h, g, wg, wu, wd, attn=None, w_o=None, *, tm=256):
    m, d = h.shape
    tm = min(tm, m)
    row = pl.BlockSpec((tm, d), lambda i: (i, 0))
    with_proj = attn is not None
    args, specs = [h], [row]
    if with_proj:
        args += [attn, w_o]
        specs += [pl.BlockSpec((tm, attn.shape[1]), lambda i: (i, 0)), _const_spec(w_o.shape)]
    args += [g.reshape(1, d), wg, wu, wd]
    specs += [_const_spec((1, d)), _const_spec(wg.shape), _const_spec(wu.shape), _const_spec(wd.shape)]
    return pl.pallas_call(
        functools.partial(_ffn_kernel, with_proj=with_proj),
        grid=(m // tm,), in_specs=specs, out_specs=row,
        out_shape=jax.ShapeDtypeStruct((m, d), F32),
        compiler_params=_cparams(1), name="ffn_layer",
    )(*args)


def _kvproj_kernel(h_ref, g_ref, w_ref, gs_ref, gw_ref, cos_ref, sdn_ref, sup_ref,
                   cmp_ref, slc_ref, win_ref, slabs_ref, *bf_refs, channel_major):
    x = h_ref[...]
    xn = _rms(x, g_ref[...]).astype(MXU_DTYPE)
    kv = _dot(xn, w_ref[...])
    width = 2 * N_KV_HEADS * HEAD_DIM
    kw = N_KV_HEADS * HEAD_DIM
    cos, sdn, sup = cos_ref[...], sdn_ref[...], sup_ref[...]

    def emit(o_ref, c, tile):
        if channel_major:
            o_ref[0, c * LANES:(c + 1) * LANES, :] = tile.T
        else:
            o_ref[:, c * LANES:(c + 1) * LANES] = tile

    for c in range(width // LANES):
        tile = kv[:, c * LANES:(c + 1) * LANES]
        emit(cmp_ref, c, tile)
        slabs_ref[c] = tile
    for bi, (base, gk_ref, o_ref) in enumerate(((width, gs_ref, slc_ref), (2 * width, gw_ref, win_ref))):
        for c in range(width // LANES):
            tile = kv[:, base + c * LANES: base + (c + 1) * LANES]
            if c < kw // LANES:
                tile = _rope128(_headnorm128(tile, gk_ref[...]), cos, sdn, sup)
            emit(o_ref, c, tile)
            if channel_major:
                bf_refs[bi][:, c * LANES:(c + 1) * LANES] = tile.astype(MXU_DTYPE)


def _kv_project(h, kv_norm, kv_w, gk_slc, gk_win, tabs, *, seq_len=None, tm=256):
    m, d = h.shape
    tm = min(tm, m)
    width = 2 * N_KV_HEADS * HEAD_DIM
    n_tab = tabs[0].shape[0] // tm
    row = pl.BlockSpec((tm, d), lambda i: (i, 0))
    tab = pl.BlockSpec((tm, LANES), lambda i: (i % n_tab, 0))
    orow = pl.BlockSpec((tm, width), lambda i: (i, 0))
    slab = pl.BlockSpec((width // LANES, tm, LANES), lambda i: (0, i, 0))
    slab_shape = jax.ShapeDtypeStruct((width // LANES, m, LANES), F32)
    if seq_len is not None:
        nblk = seq_len // tm
        ocm = pl.BlockSpec((1, width, tm), lambda i: (i // nblk, 0, i % nblk))
        out_specs = [ocm] * 3 + [slab, orow, orow]
        out_shape = ([jax.ShapeDtypeStruct((m // seq_len, width, seq_len), F32)] * 3 + [slab_shape]
                     + [jax.ShapeDtypeStruct((m, width), MXU_DTYPE)] * 2)
    else:
        out_specs = [orow] * 3 + [slab]
        out_shape = [jax.ShapeDtypeStruct((m, width), F32)] * 3 + [slab_shape]
    return pl.pallas_call(
        functools.partial(_kvproj_kernel, channel_major=seq_len is not None), grid=(m // tm,),
        in_specs=[row, _const_spec((1, d)), _const_spec(kv_w.shape), _const_spec((1, LANES)),
                  _const_spec((1, LANES)), tab, tab, tab],
        out_specs=out_specs, out_shape=out_shape,
        compiler_params=_cparams(1), name="kv_project",
    )(h, kv_norm.reshape(1, d), kv_w, gk_slc, gk_win, *tabs)


def _qproj_kernel(h_ref, g_ref, wq_ref, wgt_ref, gq_ref, cos_ref, sdn_ref, sup_ref,
                  qn_ref, qr_ref, gates_ref):
    x = h_ref[...]
    xn = _rms(x, g_ref[...]).astype(MXU_DTYPE)
    q = _dot(xn, wq_ref[...])
    gates_ref[...] = jax.nn.sigmoid(_dot(xn, wgt_ref[...])).T
    cos, sdn, sup = cos_ref[...], sdn_ref[...], sup_ref[...]
    for c in range(q.shape[1] // LANES):
        qn = _headnorm128(q[:, c * LANES:(c + 1) * LANES], gq_ref[...])
        qn_ref[:, c * LANES:(c + 1) * LANES] = (qn * ATTN_SCALE).astype(qn_ref.dtype)
        qr_ref[:, c * LANES:(c + 1) * LANES] = (_rope128(qn, cos, sdn, sup) * ATTN_SCALE).astype(qr_ref.dtype)


def _q_project(h, g, w_q, w_gate, gq, tabs, *, tm=256):
    m, d = h.shape
    tm = min(tm, m)
    dq = w_q.shape[1]
    n_tab = tabs[0].shape[0] // tm
    row = pl.BlockSpec((tm, d), lambda i: (i, 0))
    tab = pl.BlockSpec((tm, LANES), lambda i: (i % n_tab, 0))
    qrow = pl.BlockSpec((tm, dq), lambda i: (i, 0))
    return pl.pallas_call(
        _qproj_kernel, grid=(m // tm,),
        in_specs=[row, _const_spec((1, d)), _const_spec(w_q.shape), _const_spec(w_gate.shape),
                  _const_spec((1, LANES)), tab, tab, tab],
        out_specs=[qrow, qrow, pl.BlockSpec((LANES, tm), lambda i: (0, i))],
        out_shape=[jax.ShapeDtypeStruct((m, dq), MXU_DTYPE)] * 2 + [jax.ShapeDtypeStruct((LANES, m), F32)],
        compiler_params=_cparams(1), name="q_project",
    )(h, g.reshape(1, d), w_q, w_gate, gq, *tabs)


def _compress_from_slabs(x_ref, pos_ref, w_ref, o_ref):
    width = 2 * N_KV_HEADS * HEAD_DIM
    kw = N_KV_HEADS * HEAD_DIM
    mb = o_ref.shape[0]
    spk = kw // LANES
    for kv in range(2):
        acc_a = jnp.zeros((mb, kw), F32)
        acc_b = jnp.zeros((mb, kw), F32)
        for l in range(CMP_STRIDE):
            z = jnp.concatenate([x_ref[kv * spk + j, pl.ds(l, mb, stride=CMP_STRIDE), :] for j in range(spk)],
                                axis=1)
            pa = pos_ref[l:l + 1, kv * kw:(kv + 1) * kw]
            pb = pos_ref[CMP_STRIDE + l:CMP_STRIDE + l + 1, kv * kw:(kv + 1) * kw]
            acc_a = acc_a + _dot((z + pa).astype(MXU_DTYPE), w_ref[kv, l])
            acc_b = acc_b + _dot((z + pb).astype(MXU_DTYPE), w_ref[kv, CMP_STRIDE + l])
        o_ref[:, kv * kw:(kv + 1) * kw] = acc_a
        o_ref[:, width + kv * kw: width + (kv + 1) * kw] = acc_b


def _compress_pages_kernel(x_ref, pos_ref, w_ref, o_ref, xt):
    n_pg, width, page = x_ref.shape

    def body(pg, carry):
        r0 = pl.multiple_of(pg * page, page)
        for c in range(width // LANES):
            xt[c, pl.ds(r0, page), :] = x_ref[pg, c * LANES:(c + 1) * LANES, :].T
        return carry

    lax.fori_loop(0, n_pg, body, 0)
    _compress_from_slabs(xt, pos_ref, w_ref, o_ref)


def _compress_partial(slabs, pos_tab, w_bd, *, mb=256):
    n_slab, n_rows, _ = slabs.shape
    n_sub = n_rows // CMP_STRIDE
    mb = min(mb, n_sub)
    width = 2 * N_KV_HEADS * HEAD_DIM
    return pl.pallas_call(
        _compress_from_slabs, grid=(n_sub // mb,),
        in_specs=[pl.BlockSpec((n_slab, mb * CMP_STRIDE, LANES), lambda i: (0, i, 0)),
                  _const_spec(pos_tab.shape), _const_spec(w_bd.shape)],
        out_specs=pl.BlockSpec((mb, 2 * width), lambda i: (i, 0)),
        out_shape=jax.ShapeDtypeStruct((n_sub, 2 * width), F32),
        compiler_params=_cparams(1), name="compress_partial",
    )(slabs, pos_tab, w_bd)


def _compress_pages(pages_cm, pos_tab, w_bd, *, pages_per_step=32):
    n_pool, width, page = pages_cm.shape
    pps = min(pages_per_step, n_pool)
    assert n_pool % pps == 0
    mb = pps * page // CMP_STRIDE
    return pl.pallas_call(
        _compress_pages_kernel, grid=(n_pool // pps,),
        in_specs=[pl.BlockSpec((pps, width, page), lambda i: (i, 0, 0)),
                  _const_spec(pos_tab.shape), _const_spec(w_bd.shape)],
        out_specs=pl.BlockSpec((mb, 2 * width), lambda i: (i, 0)),
        out_shape=jax.ShapeDtypeStruct((n_pool * page // CMP_STRIDE, 2 * width), F32),
        scratch_shapes=[pltpu.VMEM((width // LANES, pps * page, LANES), F32)],
        compiler_params=_cparams(1), name="compress_pages",
    )(pages_cm, pos_tab, w_bd)


def _assemble_kc_vc(ab, gk):
    width = 2 * N_KV_HEADS * HEAD_DIM
    kw = N_KV_HEADS * HEAD_DIM
    n = ab.shape[0]
    kcv = ab[:, :width] + pltpu.roll(ab[:, width:], n - 1, 0)
    kc = jnp.concatenate([_headnorm128(kcv[:, c * LANES:(c + 1) * LANES], gk) for c in range(kw // LANES)],
                         axis=1)
    return kc, kcv[:, kw:width]


def _kcvc_kernel(ab_ref, gk_ref, kc_ref, vc_ref):
    kc, vc = _assemble_kc_vc(ab_ref[0], gk_ref[...])
    kc_ref[0] = kc.astype(kc_ref.dtype)
    vc_ref[0] = vc.astype(vc_ref.dtype)


def _prompt_kc_vc(ab, gk):
    b, n, w = ab.shape
    kw = N_KV_HEADS * HEAD_DIM
    spec = pl.BlockSpec((1, n, kw), lambda i: (i, 0, 0))
    return pl.pallas_call(
        _kcvc_kernel, grid=(b,),
        in_specs=[pl.BlockSpec((1, n, w), lambda i: (i, 0, 0)), _const_spec((1, LANES))],
        out_specs=[spec, spec], out_shape=[jax.ShapeDtypeStruct((b, n, kw), MXU_DTYPE)] * 2,
        compiler_params=_cparams(1), name="prompt_kc_vc",
    )(ab, gk)


def _select_blocks(imp, blk, qpos, axis, n_sel):
    cur = qpos // SEL_BLOCK
    valid = blk <= cur
    forced = (blk == 0) | (blk == cur) | (blk == cur - 1)
    sc = jnp.where(forced, jnp.inf, jnp.where(valid, imp, -jnp.inf))
    cnt = jnp.zeros(sc.shape, jnp.int32)
    for j in range(n_sel):
        sj = lax.slice_in_dim(sc, j, j + 1, axis=axis)
        beats = (sj > sc) | ((sj == sc) & (blk > j))
        cnt = cnt + beats.astype(jnp.int32)
    return (cnt < min(SEL_TOPK, n_sel)) & (sc > -jnp.inf) & (blk < n_sel)


def _dot_tn(a, b):
    return lax.dot_general(a, b, (((0,), (0,)), ((), ())), preferred_element_type=F32)


def _attn_prompt_kernel(qn_ref, qr_ref, gt_ref, kc_ref, vc_ref, slc_ref, win_ref, et_ref, mt_ref, o_ref,
                        bias_s, m_s, l_s, acc_s, *, n_sel, tk):
    tq = qn_ref.shape[0]
    t_len = slc_ref.shape[1]
    kw = N_KV_HEADS * HEAD_DIM
    rpg = 4
    ns = 2 * rpg
    qb = pl.program_id(1)
    lo = lax.broadcasted_iota(jnp.int32, (tq, LANES), 1) < HEAD_DIM
    top = lax.broadcasted_iota(jnp.int32, (LANES, tq), 0) < HEAD_DIM
    wlen = min(WINDOW + tq, t_len)

    def q_of(shape):
        return qb * tq + lax.broadcasted_iota(jnp.int32, shape, 1)

    def stack(ref, p):
        parts = []
        for side in range(2):
            keep = lo if side == 0 else jnp.logical_not(lo)
            for r in range(rpg):
                c = p * rpg + r
                t = ref[:, c * LANES:(c + 1) * LANES]
                parts.append(jnp.where(keep, t, jnp.zeros_like(t)))
        return jnp.concatenate(parts, axis=0)

    def slabs(x0, x1):
        return jnp.concatenate([x0] * rpg + [x1] * rpg, axis=1)

    for p in range(2):
        qn_st = stack(qn_ref, p)
        s = _dot_nt(kc_ref[0, :, p * LANES:(p + 1) * LANES], qn_st)
        n_cb = s.shape[0]
        cm1 = (lax.broadcasted_iota(jnp.int32, (n_cb, tq), 0) * CMP_STRIDE + (CMP_LEN - 1)) <= q_of((n_cb, tq))
        cmask = slabs(cm1, cm1)
        s = jnp.where(cmask, s, -jnp.inf)
        mx = jnp.max(s, axis=0, keepdims=True)
        mx = jnp.where(mx == -jnp.inf, 0.0, mx)
        e = jnp.where(cmask, jnp.exp(s - mx), 0.0)
        pr = e / jnp.maximum(jnp.sum(e, axis=0, keepdims=True), 1e-30)
        o_cmp = _dot_tn(vc_ref[0, :, p * LANES:(p + 1) * LANES], pr.astype(MXU_DTYPE))

        for side in range(2):
            g = 2 * p + side
            imp = pr[:, side * rpg * tq:(side * rpg + 1) * tq]
            for r in range(1, rpg):
                imp = imp + pr[:, (side * rpg + r) * tq:(side * rpg + r + 1) * tq]
            imp_t = _dot_exact_lhs(mt_ref[...], imp)[:n_sel]
            blk = lax.broadcasted_iota(jnp.int32, (n_sel, tq), 0)
            sel = _select_blocks(imp_t, blk, q_of((n_sel, tq)), 0, n_sel)
            sel_f = jnp.concatenate([jnp.where(sel, 1.0, 0.0).astype(F32),
                                     jnp.zeros((LANES - n_sel, tq), F32)], axis=0)
            sel_b = sel_f.astype(MXU_DTYPE)
            for c in range(t_len // tk):

                @pl.when(c * tk < (qb + 1) * tq)
                def _(c=c, g=g, sel_b=sel_b):
                    al = _dot(et_ref[c * tk:(c + 1) * tk, :], sel_b)
                    kpos = c * tk + lax.broadcasted_iota(jnp.int32, (tk, tq), 0)
                    bias_s[g, c * tk:(c + 1) * tk, :] = jnp.where((al > 0.5) & (kpos <= q_of((tk, tq))), 0.0, NEG)

        qr_st = stack(qr_ref, p)
        m_s[...] = jnp.full(m_s.shape, NEG, F32)
        l_s[...] = jnp.zeros(l_s.shape, F32)
        acc_s[...] = jnp.zeros(acc_s.shape, F32)

        def body(kt, carry, p=p, qr_st=qr_st):
            k0 = pl.multiple_of(kt * tk, tk)
            k = slc_ref[0, pl.ds(k0, tk), p * LANES:(p + 1) * LANES]
            v = slc_ref[0, pl.ds(k0, tk), kw + p * LANES: kw + (p + 1) * LANES]
            s = _dot_nt(k, qr_st) + slabs(bias_s[2 * p, pl.ds(k0, tk), :], bias_s[2 * p + 1, pl.ds(k0, tk), :])
            m_prev = m_s[...]
            m_new = jnp.maximum(m_prev, jnp.max(s, axis=0, keepdims=True))
            alpha = jnp.exp(m_prev - m_new)
            pe = jnp.exp(s - m_new)
            l_s[...] = alpha * l_s[...] + jnp.sum(pe, axis=0, keepdims=True)
            acc_s[...] = alpha * acc_s[...] + _dot_tn(v, pe.astype(MXU_DTYPE))
            m_s[...] = m_new
            return carry

        lax.fori_loop(0, (qb * tq + tq - 1) // tk + 1, body, 0)
        o_slc = acc_s[...] / jnp.maximum(l_s[...], 1e-30)

        k0 = pl.multiple_of(jnp.maximum(qb * tq + tq - wlen, 0), tq)
        k = win_ref[0, pl.ds(k0, wlen), p * LANES:(p + 1) * LANES]
        v = win_ref[0, pl.ds(k0, wlen), kw + p * LANES: kw + (p + 1) * LANES]
        kpos = k0 + lax.broadcasted_iota(jnp.int32, (wlen, tq), 0)
        qp = q_of((wlen, tq))
        wb = jnp.where((kpos <= qp) & (kpos > qp - WINDOW), 0.0, NEG)
        s = _dot_nt(k, qr_st) + slabs(wb, wb)
        pe = jnp.exp(s - jnp.max(s, axis=0, keepdims=True))
        o_win = _dot_tn(v, pe.astype(MXU_DTYPE)) / jnp.maximum(jnp.sum(pe, axis=0, keepdims=True), 1e-30)

        gt = gt_ref[...]
        for r in range(rpg):
            c = p * rpg + r
            h_lo = (2 * p) * rpg + r
            h_hi = (2 * p + 1) * rpg + r

            def gate(br):
                return jnp.where(top, gt[h_lo * N_BRANCH + br: h_lo * N_BRANCH + br + 1, :],
                                 gt[h_hi * N_BRANCH + br: h_hi * N_BRANCH + br + 1, :])

            def pick(x):
                return jnp.where(top, x[:, r * tq:(r + 1) * tq], x[:, (rpg + r) * tq:(rpg + r + 1) * tq])

            o = gate(0) * pick(o_cmp) + gate(1) * pick(o_slc) + gate(2) * pick(o_win)
            o_ref[:, c * LANES:(c + 1) * LANES] = o.T.astype(o_ref.dtype)


def _attn_prompt(qn, qr, gates_t, kc, vc, slc16, win16, e_t, mt, *, n_sel, tq=128, tk=512):
    b, t, w = slc16.shape
    nq = t // tq
    tk = min(tk, t)
    dq = qn.shape[1]
    n_cb = kc.shape[1]
    assert tq == LANES and t % tk == 0 and tk % tq == 0
    qspec = pl.BlockSpec((tq, dq), lambda i, j: (i * nq + j, 0))
    seq = lambda shape: pl.BlockSpec(shape, lambda i, j: (i, 0, 0))
    return pl.pallas_call(
        functools.partial(_attn_prompt_kernel, n_sel=n_sel, tk=tk),
        grid=(b, nq),
        in_specs=[qspec, qspec, pl.BlockSpec((LANES, tq), lambda i, j: (0, i * nq + j)),
                  seq((1, n_cb, kc.shape[2])), seq((1, n_cb, vc.shape[2])),
                  seq((1, t, w)), seq((1, t, w)), _const_spec(e_t.shape), _const_spec(mt.shape)],
        out_specs=qspec, out_shape=jax.ShapeDtypeStruct((b * t, dq), MXU_DTYPE),
        scratch_shapes=[pltpu.VMEM((N_KV_HEADS, t, tq), F32),
                        pltpu.VMEM((1, 8 * tq), F32), pltpu.VMEM((1, 8 * tq), F32),
                        pltpu.VMEM((LANES, 8 * tq), F32)],
        compiler_params=_cparams(2), name="attn_prompt",
    )(qn, qr, gates_t, kc, vc, slc16, win16, e_t, mt)


def _attn_sample_kernel(pt_ref, qn_ref, qr_ref, gt_ref, gk_ref, rsum_ref, m_ref, e_ref, abn_ref, sn_ref,
                        wst_ref, wn_ref, *rest, n_pages, n_sel, past_len, t_new, emit_window):
    ab_pages = rest[:n_pages]
    slc_pages = rest[n_pages:2 * n_pages]
    if emit_window:
        o_ref, wout_ref, pad_s = rest[2 * n_pages:]
    else:
        (o_ref, pad_s), wout_ref = rest[2 * n_pages:], None
    kw = N_KV_HEADS * HEAD_DIM
    page = slc_pages[0].shape[2]
    nrow = qn_ref.shape[1]
    rows_per_g = nrow // N_KV_HEADS
    qn = qn_ref[0]
    qr = qr_ref[0]
    t_of_row = lax.broadcasted_iota(jnp.int32, (nrow, 1), 0) % t_new
    qpos = past_len + t_of_row

    def softmax(s, mask):
        s = jnp.where(mask, s, -jnp.inf)
        mx = jnp.max(s, axis=-1, keepdims=True)
        mx = jnp.where(mx == -jnp.inf, 0.0, mx)
        e = jnp.where(mask, jnp.exp(s - mx), 0.0)
        return e / jnp.maximum(jnp.sum(e, axis=-1, keepdims=True), 1e-30)

    n_ab = m_ref.shape[0]
    pieces = [r[0] for r in ab_pages] + [abn_ref[0]]
    n_have = sum(x.shape[0] for x in pieces)
    pieces.append(jnp.zeros((n_ab - n_have, pieces[0].shape[1]), F32))
    kc, vc = _assemble_kc_vc(jnp.concatenate(pieces, axis=0), gk_ref[...])
    blk_c = lax.broadcasted_iota(jnp.int32, (nrow, n_ab), 1)
    cmask = (blk_c * CMP_STRIDE + (CMP_LEN - 1)) <= qpos
    pc = softmax(_dot_nt(qn, kc.astype(MXU_DTYPE)), cmask)
    o_cmp = _dot(pc.astype(MXU_DTYPE), vc.astype(MXU_DTYPE))

    imp = _dot_exact_lhs(rsum_ref[...], pc)
    imp_s = _dot_exact_rhs(imp, m_ref[...])
    blk = lax.broadcasted_iota(jnp.int32, imp_s.shape, 1)
    sel = _select_blocks(imp_s, blk, jnp.broadcast_to(qpos, imp_s.shape), 1, n_sel)
    al = _dot(jnp.where(sel, 1.0, 0.0).astype(MXU_DTYPE), e_ref[...])
    n_keys = al.shape[1]
    kpos = lax.broadcasted_iota(jnp.int32, (nrow, n_keys), 1)
    smask = (al > 0.5) & (kpos <= qpos)

    pad_s[...] = jnp.zeros(pad_s.shape, F32)
    pad_s[0:sn_ref.shape[1], :] = sn_ref[0]
    new_rows = pad_s[...]
    s = jnp.concatenate([_dot(qr, r[0, :kw, :].astype(MXU_DTYPE)) for r in slc_pages]
                        + [_dot_nt(qr, new_rows[:, :kw].astype(MXU_DTYPE))], axis=1)
    ps = softmax(s, smask).astype(MXU_DTYPE)
    o_slc = _dot(ps[:, n_pages * page:], new_rows[:, kw:].astype(MXU_DTYPE))
    for i, r in enumerate(slc_pages):
        o_slc = o_slc + _dot_nt(ps[:, i * page:(i + 1) * page], r[0, kw:, :].astype(MXU_DTYPE))

    wbuf = wst_ref.shape[2]
    pad_s[...] = jnp.zeros(pad_s.shape, F32)
    pad_s[0:wn_ref.shape[1], :] = wn_ref[0]
    new_rows = pad_s[...]
    s = jnp.concatenate([_dot(qr, wst_ref[0, :kw, :].astype(MXU_DTYPE)),
                         _dot_nt(qr, new_rows[:, :kw].astype(MXU_DTYPE))], axis=1)
    wpos = (past_len - wbuf) + lax.broadcasted_iota(jnp.int32, s.shape, 1)
    wmask = (wpos <= qpos) & (wpos > qpos - WINDOW) & (wpos >= 0)
    pw = softmax(s, wmask).astype(MXU_DTYPE)
    o_win = (_dot_nt(pw[:, :wbuf], wst_ref[0, kw:, :].astype(MXU_DTYPE))
             + _dot(pw[:, wbuf:], new_rows[:, kw:].astype(MXU_DTYPE)))

    gt = gt_ref[0]
    o = gt[:, 0:1] * o_cmp + gt[:, 1:2] * o_slc + gt[:, 2:3] * o_win
    lane_g = lax.broadcasted_iota(jnp.int32, (rows_per_g, kw), 1) // HEAD_DIM
    acc = jnp.zeros((rows_per_g, kw), F32)
    for g in range(N_KV_HEADS):
        acc = acc + jnp.where(lane_g == g, o[g * rows_per_g:(g + 1) * rows_per_g], 0.0)
    o_ref[0] = acc

    if wout_ref is not None:
        rolled = pltpu.roll(wst_ref[0], wbuf - t_new, 1)
        wout_ref[0, :, 0:wbuf - LANES] = rolled[:, 0:wbuf - LANES]
        new_t = pltpu.roll(new_rows.T, LANES - t_new, 1)
        ln = lax.broadcasted_iota(jnp.int32, new_t.shape, 1)
        wout_ref[0, :, wbuf - LANES:wbuf] = jnp.where(ln >= LANES - t_new, new_t, rolled[:, wbuf - LANES:wbuf])


def _attn_sample(page_table, qn, qr, gates, gk, rsum, m_mat, e_mat, ab_pool, ab_new, cache_slc, slc_new,
                 win_state, win_new, *, n_sel, past_len, t_new, emit_window):
    nb, n_pages = page_table.shape
    nrow, kw = qn.shape[1], qn.shape[2]
    width = cache_slc.shape[1]
    page = cache_slc.shape[2]
    wbuf = win_state.shape[2]
    per_b = lambda shape: pl.BlockSpec(shape, lambda i, pt: (i, 0, 0))

    def page_spec(shape, p):
        return pl.BlockSpec(shape, lambda i, pt: (pt[i * n_pages + p], 0, 0))

    in_specs = [per_b((1, nrow, kw)), per_b((1, nrow, kw)), per_b((1, nrow, N_BRANCH)),
                pl.BlockSpec((1, LANES), lambda i, pt: (0, 0)),
                pl.BlockSpec(rsum.shape, lambda i, pt: (0, 0)),
                pl.BlockSpec(m_mat.shape, lambda i, pt: (0, 0)),
                pl.BlockSpec(e_mat.shape, lambda i, pt: (0, 0)),
                per_b((1,) + ab_new.shape[1:]), per_b((1,) + slc_new.shape[1:]),
                per_b((1, width, wbuf)), per_b((1,) + win_new.shape[1:])]
    in_specs += [page_spec((1,) + ab_pool.shape[1:], p) for p in range(n_pages)]
    in_specs += [page_spec((1, width, page), p) for p in range(n_pages)]
    out_specs = [per_b((1, nrow // N_KV_HEADS, kw))]
    out_shape = [jax.ShapeDtypeStruct((nb, nrow // N_KV_HEADS, kw), F32)]
    if emit_window:
        out_specs.append(per_b((1, width, wbuf)))
        out_shape.append(jax.ShapeDtypeStruct((nb, width, wbuf), F32))
    grid_spec = pltpu.PrefetchScalarGridSpec(
        num_scalar_prefetch=1, grid=(nb,), in_specs=in_specs, out_specs=out_specs,
        scratch_shapes=[pltpu.VMEM((page, width), F32)])
    res = pl.pallas_call(
        functools.partial(_attn_sample_kernel, n_pages=n_pages, n_sel=n_sel, past_len=past_len, t_new=t_new,
                          emit_window=emit_window),
        grid_spec=grid_spec, out_shape=out_shape,
        compiler_params=_cparams(1), name="attn_sample",
    )(page_table.reshape(-1), qn, qr, gates, gk, rsum, m_mat, e_mat, ab_new, slc_new, win_state, win_new,
      *([ab_pool] * n_pages), *([cache_slc] * n_pages))
    return (res[0], res[1]) if emit_window else (res[0], None)


def _rope_tables(pos):
    half = ROPE_DIM // 2
    inv_freq = ROPE_THETA ** (-jnp.arange(half, dtype=F32) / half)
    ang = pos.astype(F32)[:, None] * inv_freq[None, :]
    cos, sin = jnp.cos(ang), jnp.sin(ang)
    n = pos.shape[0]
    z = jnp.zeros((n, HEAD_DIM - ROPE_DIM), F32)
    zh = jnp.zeros((n, half), F32)
    c_head = jnp.concatenate([cos, cos, z + 1.0], axis=1)
    dn_head = jnp.concatenate([zh, sin, z], axis=1)
    up_head = jnp.concatenate([-sin, zh, z], axis=1)
    tile2 = lambda a: jnp.concatenate([a, a], axis=1)
    return tile2(c_head), tile2(dn_head), tile2(up_head)


def _sel_matrices(n_cmp, n_cmp_pad, n_sel, n_keys):
    i = jnp.arange(n_cmp_pad)[:, None]
    j = jnp.arange(LANES)[None, :]
    m = ((i * CMP_STRIDE < j * SEL_BLOCK + SEL_BLOCK) & (i * CMP_STRIDE + CMP_LEN > j * SEL_BLOCK)
         & (i < n_cmp) & (j < n_sel))
    e = (jnp.arange(n_keys)[None, :] // SEL_BLOCK == jnp.arange(LANES)[:, None]) & (jnp.arange(LANES)[:, None] < n_sel)
    return m.astype(MXU_DTYPE), e.astype(MXU_DTYPE)


def _tile2(g):
    return jnp.concatenate([g, g]).reshape(1, LANES).astype(F32)


def kernel(x_prompt, x_sample, cache_kv_cmp, cache_kv_slc, state_kv_win, page_table, norm_mix, norm_ffn,
           ffn_w_gate, ffn_w_up, ffn_w_down, a_w_in, a_ln_g, a_ln_b, a_w_spatial, a_b_spatial, a_w_out,
           kv_norm, kv_w, k_norm_cmp, k_norm_slc, k_norm_win, cmp_pos_k, cmp_pos_v, cmp_w_k, cmp_w_v,
           b_w_in, b_q_norm, b_w_out):
    bf = MXU_DTYPE
    depth = norm_mix.shape[0]
    n_a = a_w_in.shape[0]
    bsz, t_p, d_model = x_prompt.shape
    nb, t_s, _ = x_sample.shape
    n_pages = page_table.shape[1]
    page = cache_kv_cmp.shape[1]
    past_len = n_pages * page
    width = 2 * N_KV_HEADS * HEAD_DIM
    kw = N_KV_HEADS * HEAD_DIM
    n_heads = b_w_out.shape[1] // HEAD_DIM
    rpg = n_heads // N_KV_HEADS
    dq = n_heads * HEAD_DIM
    assert N_KV_HEADS == 4 and rpg == 4 and t_p % CHUNK == 0 and CHUNK % t_s == 0 and t_s <= 8

    wg, wu, wd = ffn_w_gate.astype(bf), ffn_w_up.astype(bf), ffn_w_down.astype(bf)
    a_in, a_out = a_w_in.astype(bf), a_w_out.astype(bf)
    tril = jnp.tril(jnp.ones((CHUNK, CHUNK), bool))
    ws_p = jnp.where(tril[None, None], a_w_spatial, 0.0).astype(bf)
    bs_p = jnp.swapaxes(a_b_spatial, 1, 2)
    rep = CHUNK // t_s
    ws4 = jnp.where(tril[None, None, :t_s, :t_s], a_w_spatial[:, :, :t_s, :t_s], 0.0)
    ws_s = jnp.einsum('ab,lgts->lgatbs', jnp.eye(rep, dtype=F32), ws4).reshape(n_a, A_GROUPS, CHUNK, CHUNK).astype(bf)
    bs_s = jnp.tile(jnp.swapaxes(a_b_spatial[:, :, :t_s], 1, 2), (1, rep, 1))
    perm = jnp.array([(2 * p + side) * rpg + r for p in range(2) for r in range(rpg) for side in range(2)])
    wq = b_w_in[:, :, :dq].reshape(-1, d_model, n_heads, HEAD_DIM)[:, :, perm].reshape(-1, d_model, dq).astype(bf)
    wgate = jnp.pad(b_w_in[:, :, dq:], ((0, 0), (0, 0), (0, LANES - N_BRANCH * n_heads))).astype(bf)
    wo_pair = b_w_out.reshape(-1, n_heads, HEAD_DIM, d_model)[:, perm].reshape(-1, dq, d_model).astype(bf)
    wo_flat = b_w_out.astype(bf)
    kvw = kv_w.astype(bf)
    eye_g = jnp.eye(N_KV_HEADS, dtype=F32)
    w_bd = jnp.stack([jnp.einsum('ab,lde->ladbe', eye_g, w).reshape(CMP_LEN, kw, kw)
                      for w in (cmp_w_k, cmp_w_v)]).astype(bf)
    pos_tab = jnp.concatenate([jnp.tile(cmp_pos_k, (1, N_KV_HEADS)), jnp.tile(cmp_pos_v, (1, N_KV_HEADS))], axis=1)
    gk_cmp, gk_slc, gk_win = _tile2(k_norm_cmp), _tile2(k_norm_slc), _tile2(k_norm_win)

    def dense_a_layers(h, ws_eff, bs_eff, emit_v):
        vs = []
        for l in range(n_a):
            h, v = _gmlp_layer(h, norm_mix[l], a_in[l], a_ln_g[l], a_ln_b[l], ws_eff[l], bs_eff[l], a_out[l],
                               emit_v=emit_v)
            vs.append(v)
            h = _ffn_layer(h, norm_ffn[l], wg[l], wu[l], wd[l])
        return h, vs

    tabs_p = _rope_tables(jnp.arange(t_p))
    h = x_prompt.reshape(bsz * t_p, d_model)
    h, _ = dense_a_layers(h, ws_p, bs_p, False)
    kv_cmp_p, kv_slc_p, kv_win_p, cmp_slabs, slc16, win16 = _kv_project(h, kv_norm, kvw, gk_slc, gk_win, tabs_p,
                                                                        seq_len=t_p)
    ab = _compress_partial(cmp_slabs, pos_tab, w_bd)
    n_sub = t_p // CMP_STRIDE
    kc, vc = _prompt_kc_vc(ab.reshape(bsz, n_sub, 2 * width), gk_cmp)
    n_sel_p = t_p // SEL_BLOCK
    m_p, e_p = _sel_matrices(n_sub - 1, n_sub, n_sel_p, t_p)
    for j in range(depth - n_a):
        l = n_a + j
        qn, qr, gates = _q_project(h, norm_mix[l], wq[j], wgate[j], _tile2(b_q_norm[j]), tabs_p)
        o = _attn_prompt(qn, qr, gates, kc, vc, slc16.reshape(bsz, t_p, width), win16.reshape(bsz, t_p, width),
                         e_p.T, m_p.T, n_sel=n_sel_p)
        h = _ffn_layer(h, norm_ffn[l], wg[l], wu[l], wd[l], attn=o, w_o=wo_pair[j])
    y_prompt = h.reshape(bsz, t_p, d_model)
    row5 = lambda a, b_, t_: a.reshape(b_, t_, 2, N_KV_HEADS, HEAD_DIM)
    from_cm = lambda a: jnp.transpose(a.reshape(a.shape[0], 2, N_KV_HEADS, HEAD_DIM, a.shape[2]), (0, 4, 1, 2, 3))
    to_cm = lambda a: jnp.transpose(a, (0, 2, 3, 4, 1)).reshape(a.shape[0], width, a.shape[1])
    w_keep = min(WINDOW, t_p)
    kv_win_prompt = from_cm(kv_win_p[:, :, t_p - w_keep:])

    m_s = nb * t_s
    tabs_s = _rope_tables(past_len + (jnp.arange(nb * t_s) % t_s))
    hs = x_sample.reshape(m_s, d_model)
    hs, vs = dense_a_layers(hs, ws_s, bs_s, True)
    a_v_sample = jnp.stack(vs).reshape(n_a, nb, t_s, -1)
    kv_cmp_s, kv_slc_s, kv_win_s, cmp_slabs_s = _kv_project(hs, kv_norm, kvw, gk_slc, gk_win, tabs_s)
    n_pool = cache_kv_cmp.shape[0]
    sub_pp = page // CMP_STRIDE
    ab_pool = _compress_pages(to_cm(cache_kv_cmp), pos_tab, w_bd).reshape(n_pool, sub_pp, 2 * width)
    t_full = -(-(past_len + t_s) // SEL_BLOCK) * SEL_BLOCK
    n_new_sub = (t_full - past_len) // CMP_STRIDE
    new_cmp = jnp.pad(cmp_slabs_s.reshape(-1, nb, t_s, LANES), ((0, 0), (0, 0), (0, t_full - past_len - t_s), (0, 0)))
    ab_new = _compress_partial(new_cmp.reshape(-1, nb * (t_full - past_len), LANES), pos_tab, w_bd)
    ab_new = jnp.pad(ab_new.reshape(nb, n_new_sub, 2 * width), ((0, 0), (0, 8 - n_new_sub), (0, 0)))
    n_cmp_s = t_full // CMP_STRIDE - 1
    n_sel_s = t_full // SEL_BLOCK
    n_cmp_pad = 256
    n_keys = past_len + page
    m_sm, e_sm = _sel_matrices(n_cmp_s, n_cmp_pad, n_sel_s, n_keys)
    nrow = n_heads * t_s
    ridx = jnp.arange(nrow)
    rsum = ((ridx[:, None] // (rpg * t_s) == ridx[None, :] // (rpg * t_s))
            & (ridx[:, None] % t_s == ridx[None, :] % t_s)).astype(bf)
    pad8 = lambda a: jnp.pad(a.reshape(nb, t_s, width), ((0, 0), (0, 8 - t_s), (0, 0)))
    slc_new, win_new = pad8(kv_slc_s), pad8(kv_win_s)
    inv_perm = jnp.argsort(perm)

    def pad_q(q):
        q = q.reshape(nb, t_s, n_heads, HEAD_DIM)[:, :, inv_perm].reshape(nb, t_s, N_KV_HEADS, rpg, HEAD_DIM)
        q = jnp.transpose(q, (0, 2, 3, 1, 4))
        q = jnp.einsum('bgrtd,gk->bgrtkd', q, jnp.eye(N_KV_HEADS, dtype=q.dtype))
        return q.reshape(nb, nrow, kw)

    slc_cm, win_cm = to_cm(cache_kv_slc), to_cm(state_kv_win)
    kv_win_sample = None
    for j in range(depth - n_a):
        l = n_a + j
        qn, qr, gates = _q_project(hs, norm_mix[l], wq[j], wgate[j], _tile2(b_q_norm[j]), tabs_s)
        g3 = gates[:N_BRANCH * n_heads].T.reshape(nb, t_s, N_KV_HEADS, rpg, N_BRANCH)
        g3 = jnp.transpose(g3, (0, 2, 3, 1, 4)).reshape(nb, nrow, N_BRANCH)
        o, w_out = _attn_sample(page_table, pad_q(qn), pad_q(qr), g3, gk_cmp, rsum, m_sm, e_sm,
                                ab_pool, ab_new, slc_cm, slc_new, win_cm, win_new,
                                n_sel=n_sel_s, past_len=past_len, t_new=t_s, emit_window=j == 0)
        if j == 0:
            kv_win_sample = from_cm(w_out)
        o = jnp.transpose(o.reshape(nb, rpg, t_s, N_KV_HEADS, HEAD_DIM), (0, 2, 3, 1, 4)).reshape(m_s, dq)
        hs = _ffn_layer(hs, norm_ffn[l], wg[l], wu[l], wd[l], attn=o.astype(bf), w_o=wo_flat[j])
    y_sample = hs.reshape(nb, t_s, d_model)

    return (y_prompt, y_sample, from_cm(kv_cmp_p), row5(kv_cmp_s, nb, t_s),
            from_cm(kv_slc_p), row5(kv_slc_s, nb, t_s), kv_win_prompt, kv_win_sample, a_v_sample)
```

```python
import functools

import jax
import jax.numpy as jnp
from jax import lax
from jax.experimental import pallas as pl
from jax.experimental.pallas import tpu as pltpu

CHUNK = 128
A_GROUPS = 8
HEAD_DIM = 64
N_KV_HEADS = 4
ROPE_DIM = HEAD_DIM // 4
ROPE_THETA = 500000.0
CMP_STRIDE = 16
CMP_LEN = 2 * CMP_STRIDE
SEL_BLOCK = 64
SEL_TOPK = 16
WINDOW = 512
N_BRANCH = 3
EPS = 1e-6
ATTN_SCALE = HEAD_DIM ** -0.5

LANES = 128
PAGE_SUB_PITCH = 24
VMEM_LIMIT_BYTES = 56 * 1024 * 1024
MXU_DTYPE = jnp.bfloat16
NEG = -1e30
F32 = jnp.float32


def _cparams(n_grid):
    return pltpu.CompilerParams(dimension_semantics=("arbitrary",) * n_grid,
                                vmem_limit_bytes=VMEM_LIMIT_BYTES)


def _const_spec(shape):
    nd = len(shape)
    return pl.BlockSpec(shape, lambda *_: (0,) * nd, pipeline_mode=pl.Buffered(1))


def _dot(a, b):
    return jnp.dot(a, b, preferred_element_type=F32)


def _dot_nt(a, b):
    return lax.dot_general(a, b, (((1,), (1,)), ((), ())), preferred_element_type=F32)


def _split3(x):
    hi = x.astype(MXU_DTYPE)
    r1 = x - hi.astype(F32)
    mid = r1.astype(MXU_DTYPE)
    lo = (r1 - mid.astype(F32)).astype(MXU_DTYPE)
    return hi, mid, lo


def _dot_exact_lhs(a, x):
    hi, mid, lo = _split3(x)
    return _dot(a, hi) + _dot(a, mid) + _dot(a, lo)


def _dot_exact_rhs(x, b):
    hi, mid, lo = _split3(x)
    return _dot(hi, b) + _dot(mid, b) + _dot(lo, b)


def _rms(x, g):
    return x * lax.rsqrt(jnp.mean(x * x, axis=-1, keepdims=True) + EPS) * g


def _headnorm128(x, g):
    lo = lax.broadcasted_iota(jnp.int32, x.shape, 1) < HEAD_DIM
    sq = x * x
    s_lo = jnp.sum(jnp.where(lo, sq, 0.0), axis=-1, keepdims=True)
    s_hi = jnp.sum(jnp.where(lo, 0.0, sq), axis=-1, keepdims=True)
    ms = jnp.where(lo, s_lo, s_hi) * (1.0 / HEAD_DIM)
    return x * lax.rsqrt(ms + EPS) * g


def _rope128(x, cos, s_dn, s_up):
    half = ROPE_DIM // 2
    return x * cos + pltpu.roll(x, half, 1) * s_dn + pltpu.roll(x, LANES - half, 1) * s_up


def _gmlp_kernel(h_ref, g_ref, win_ref, lng_ref, lnb_ref, ws_ref, bs_ref, wout_ref, *rest, emit_v):
    if emit_v:
        o_ref, v_ref, gated = rest
    else:
        o_ref, gated = rest
    tm = h_ref.shape[0]
    d_a = lng_ref.shape[1]
    dg = d_a // A_GROUPS
    x = h_ref[...]
    xn = _rms(x, g_ref[...]).astype(MXU_DTYPE)
    u = jax.nn.gelu(_dot(xn, win_ref[:, :d_a]))
    v = jax.nn.gelu(_dot(xn, win_ref[:, d_a:]))
    vc = v - jnp.mean(v, axis=-1, keepdims=True)
    v = vc * lax.rsqrt(jnp.mean(vc * vc, axis=-1, keepdims=True) + EPS) * lng_ref[...] + lnb_ref[...]
    if emit_v:
        v_ref[...] = v
    vb = v.astype(MXU_DTYPE)
    for c in range(tm // CHUNK):
        r0, r1 = c * CHUNK, (c + 1) * CHUNK
        for g in range(A_GROUPS):
            sv = _dot(ws_ref[g], vb[r0:r1, g * dg:(g + 1) * dg]) + bs_ref[:, g:g + 1]
            gated[r0:r1, g * dg:(g + 1) * dg] = (u[r0:r1, g * dg:(g + 1) * dg] * sv).astype(MXU_DTYPE)
    o_ref[...] = x + _dot(gated[...], wout_ref[...])


def _gmlp_layer(h, g, w_in, ln_g, ln_b, ws_eff, bs_eff, w_out, *, emit_v, tm=256):
    m, d = h.shape
    tm = min(tm, m)
    d_a = ln_g.shape[-1]
    row = pl.BlockSpec((tm, d), lambda i: (i, 0))
    out_shape = [jax.ShapeDtypeStruct((m, d), F32)]
    out_specs = [row]
    if emit_v:
        out_shape.append(jax.ShapeDtypeStruct((m, d_a), F32))
        out_specs.append(pl.BlockSpec((tm, d_a), lambda i: (i, 0)))
    res = pl.pallas_call(
        functools.partial(_gmlp_kernel, emit_v=emit_v),
        grid=(m // tm,),
        in_specs=[row, _const_spec((1, d)), _const_spec(w_in.shape), _const_spec((1, d_a)),
                  _const_spec((1, d_a)), _const_spec(ws_eff.shape), _const_spec(bs_eff.shape),
                  _const_spec(w_out.shape)],
        out_specs=out_specs, out_shape=out_shape,
        scratch_shapes=[pltpu.VMEM((tm, d_a), MXU_DTYPE)],
        compiler_params=_cparams(1), name="gmlp_layer",
    )(h, g.reshape(1, d), w_in, ln_g.reshape(1, d_a), ln_b.reshape(1, d_a), ws_eff, bs_eff, w_out)
    return (res[0], res[1]) if emit_v else (res[0], None)


def _ffn_kernel(*refs, with_proj):
    if with_proj:
        h_ref, a_ref, wo_ref, g_ref, wg_ref, wu_ref, wd_ref, o_ref = refs
        x = h_ref[...] + _dot(a_ref[...], wo_ref[...])
    else:
        h_ref, g_ref, wg_ref, wu_ref, wd_ref, o_ref = refs
        x = h_ref[...]
    xn = _rms(x, g_ref[...]).astype(MXU_DTYPE)
    mid = (jax.nn.silu(_dot(xn, wg_ref[...])) * _dot(xn, wu_ref[...])).astype(MXU_DTYPE)
    o_ref[...] = x + _dot(mid, wd_ref[...])


def _ffn_layer(h, g, wg, wu, wd, attn=None, w_o=None, *, tm=512):
    m, d = h.shape
    tm = min(tm, m)
    row = pl.BlockSpec((tm, d), lambda i: (i, 0))
    with_proj = attn is not None
    args, specs = [h], [row]
    if with_proj:
        args += [attn, w_o]
        specs += [pl.BlockSpec((tm, attn.shape[1]), lambda i: (i, 0)), _const_spec(w_o.shape)]
    args += [g.reshape(1, d), wg, wu, wd]
    specs += [_const_spec((1, d)), _const_spec(wg.shape), _const_spec(wu.shape), _const_spec(wd.shape)]
    return pl.pallas_call(
        functools.partial(_ffn_kernel, with_proj=with_proj),
        grid=(m // tm,), in_specs=specs, out_specs=row,
        out_shape=jax.ShapeDtypeStruct((m, d), F32),
        compiler_params=_cparams(1), name="ffn_layer",
    )(*args)


def _kvproj_kernel(h_ref, g_ref, w_ref, gs_ref, gw_ref, cos_ref, sdn_ref, sup_ref,
                   cmp_ref, slc_ref, win_ref, slabs_ref, *bf_refs, channel_major):
    x = h_ref[...]
    xn = _rms(x, g_ref[...]).astype(MXU_DTYPE)
    kv = _dot(xn, w_ref[...])
    width = 2 * N_KV_HEADS * HEAD_DIM
    kw = N_KV_HEADS * HEAD_DIM
    cos, sdn, sup = cos_ref[...], sdn_ref[...], sup_ref[...]

    def emit(o_ref, c, tile):
        if channel_major:
            o_ref[0, c * LANES:(c + 1) * LANES, :] = tile.T
        else:
            o_ref[:, c * LANES:(c + 1) * LANES] = tile

    for c in range(width // LANES):
        tile = kv[:, c * LANES:(c + 1) * LANES]
        emit(cmp_ref, c, tile)
        slabs_ref[c] = tile
    for bi, (base, gk_ref, o_ref) in enumerate(((width, gs_ref, slc_ref), (2 * width, gw_ref, win_ref))):
        for c in range(width // LANES):
            tile = kv[:, base + c * LANES: base + (c + 1) * LANES]
            if c < kw // LANES:
                tile = _rope128(_headnorm128(tile, gk_ref[...]), cos, sdn, sup)
            emit(o_ref, c, tile)
            if channel_major:
                bf_refs[bi][:, c * LANES:(c + 1) * LANES] = tile.astype(MXU_DTYPE)


def _kv_project(h, kv_norm, kv_w, gk_slc, gk_win, tabs, *, seq_len=None, tm=256):
    m, d = h.shape
    tm = min(tm, m)
    width = 2 * N_KV_HEADS * HEAD_DIM
    n_tab = tabs[0].shape[0] // tm
    row = pl.BlockSpec((tm, d), lambda i: (i, 0))
    tab = pl.BlockSpec((tm, LANES), lambda i: (i % n_tab, 0))
    orow = pl.BlockSpec((tm, width), lambda i: (i, 0))
    slab = pl.BlockSpec((width // LANES, tm, LANES), lambda i: (0, i, 0))
    slab_shape = jax.ShapeDtypeStruct((width // LANES, m, LANES), F32)
    if seq_len is not None:
        nblk = seq_len // tm
        ocm = pl.BlockSpec((1, width, tm), lambda i: (i // nblk, 0, i % nblk))
        out_specs = [ocm] * 3 + [slab, orow, orow]
        out_shape = ([jax.ShapeDtypeStruct((m // seq_len, width, seq_len), F32)] * 3 + [slab_shape]
                     + [jax.ShapeDtypeStruct((m, width), MXU_DTYPE)] * 2)
    else:
        out_specs = [orow] * 3 + [slab]
        out_shape = [jax.ShapeDtypeStruct((m, width), F32)] * 3 + [slab_shape]
    return pl.pallas_call(
        functools.partial(_kvproj_kernel, channel_major=seq_len is not None), grid=(m // tm,),
        in_specs=[row, _const_spec((1, d)), _const_spec(kv_w.shape), _const_spec((1, LANES)),
                  _const_spec((1, LANES)), tab, tab, tab],
        out_specs=out_specs, out_shape=out_shape,
        compiler_params=_cparams(1), name="kv_project",
    )(h, kv_norm.reshape(1, d), kv_w, gk_slc, gk_win, *tabs)


def _qproj_kernel(h_ref, g_ref, wq_ref, wgt_ref, gq_ref, cos_ref, sdn_ref, sup_ref,
                  qn_ref, qr_ref, gates_ref):
    x = h_ref[...]
    xn = _rms(x, g_ref[...]).astype(MXU_DTYPE)
    q = _dot(xn, wq_ref[...])
    gates_ref[...] = jax.nn.sigmoid(_dot(xn, wgt_ref[...])).T
    cos, sdn, sup = cos_ref[...], sdn_ref[...], sup_ref[...]
    for c in range(q.shape[1] // LANES):
        qn = _headnorm128(q[:, c * LANES:(c + 1) * LANES], gq_ref[...])
        qn_ref[:, c * LANES:(c + 1) * LANES] = qn.astype(qn_ref.dtype)
        qr_ref[:, c * LANES:(c + 1) * LANES] = _rope128(qn, cos, sdn, sup).astype(qr_ref.dtype)


def _q_project(h, g, w_q, w_gate, gq, tabs, *, tm=256):
    m, d = h.shape
    tm = min(tm, m)
    dq = w_q.shape[1]
    n_tab = tabs[0].shape[0] // tm
    row = pl.BlockSpec((tm, d), lambda i: (i, 0))
    tab = pl.BlockSpec((tm, LANES), lambda i: (i % n_tab, 0))
    qrow = pl.BlockSpec((tm, dq), lambda i: (i, 0))
    return pl.pallas_call(
        _qproj_kernel, grid=(m // tm,),
        in_specs=[row, _const_spec((1, d)), _const_spec(w_q.shape), _const_spec(w_gate.shape),
                  _const_spec((1, LANES)), tab, tab, tab],
        out_specs=[qrow, qrow, pl.BlockSpec((LANES, tm), lambda i: (0, i))],
        out_shape=[jax.ShapeDtypeStruct((m, dq), MXU_DTYPE)] * 2 + [jax.ShapeDtypeStruct((LANES, m), F32)],
        compiler_params=_cparams(1), name="q_project",
    )(h, g.reshape(1, d), w_q, w_gate, gq, *tabs)


def _compress_from_slabs(x_ref, pos_ref, w_ref, o_ref, pitch=CMP_STRIDE):
    width = 2 * N_KV_HEADS * HEAD_DIM
    kw = N_KV_HEADS * HEAD_DIM
    mb = o_ref.shape[0]
    spk = kw // LANES
    for kv in range(2):
        acc_a = jnp.zeros((mb, kw), F32)
        acc_b = jnp.zeros((mb, kw), F32)
        for l in range(CMP_STRIDE):
            z = jnp.concatenate([x_ref[kv * spk + j, pl.ds(l, mb, stride=pitch), :] for j in range(spk)], axis=1)
            pa = pos_ref[l:l + 1, kv * kw:(kv + 1) * kw]
            pb = pos_ref[CMP_STRIDE + l:CMP_STRIDE + l + 1, kv * kw:(kv + 1) * kw]
            acc_a = acc_a + _dot((z + pa).astype(MXU_DTYPE), w_ref[kv, l])
            acc_b = acc_b + _dot((z + pb).astype(MXU_DTYPE), w_ref[kv, CMP_STRIDE + l])
        o_ref[:, kv * kw:(kv + 1) * kw] = acc_a
        o_ref[:, width + kv * kw: width + (kv + 1) * kw] = acc_b


def _compress_pages_kernel(x_ref, pos_ref, w_ref, o_ref, xt):
    n_pg, width, page = x_ref.shape
    sub_pp = page // CMP_STRIDE

    def body(pg, carry):
        r0 = pl.multiple_of(pg * (sub_pp * PAGE_SUB_PITCH), 8)
        for c in range(width // LANES):
            rows = x_ref[pg, c * LANES:(c + 1) * LANES, :].T
            for m in range(sub_pp):
                xt[c, pl.ds(r0 + m * PAGE_SUB_PITCH, CMP_STRIDE), :] = rows[m * CMP_STRIDE:(m + 1) * CMP_STRIDE]
        return carry

    lax.fori_loop(0, n_pg, body, 0, unroll=4)
    _compress_from_slabs(xt, pos_ref, w_ref, o_ref, pitch=PAGE_SUB_PITCH)


def _compress_partial(slabs, pos_tab, w_bd, *, mb=256):
    n_slab, n_rows, _ = slabs.shape
    n_sub = n_rows // CMP_STRIDE
    mb = min(mb, n_sub)
    width = 2 * N_KV_HEADS * HEAD_DIM
    return pl.pallas_call(
        _compress_from_slabs, grid=(n_sub // mb,),
        in_specs=[pl.BlockSpec((n_slab, mb * CMP_STRIDE, LANES), lambda i: (0, i, 0)),
                  _const_spec(pos_tab.shape), _const_spec(w_bd.shape)],
        out_specs=pl.BlockSpec((mb, 2 * width), lambda i: (i, 0)),
        out_shape=jax.ShapeDtypeStruct((n_sub, 2 * width), F32),
        compiler_params=_cparams(1), name="compress_partial",
    )(slabs, pos_tab, w_bd)


def _compress_pages(pages_cm, pos_tab, w_bd, *, pages_per_step=32):
    n_pool, width, page = pages_cm.shape
    pps = min(pages_per_step, n_pool)
    assert n_pool % pps == 0
    mb = pps * page // CMP_STRIDE
    return pl.pallas_call(
        _compress_pages_kernel, grid=(n_pool // pps,),
        in_specs=[pl.BlockSpec((pps, width, page), lambda i: (i, 0, 0)),
                  _const_spec(pos_tab.shape), _const_spec(w_bd.shape)],
        out_specs=pl.BlockSpec((mb, 2 * width), lambda i: (i, 0)),
        out_shape=jax.ShapeDtypeStruct((n_pool * page // CMP_STRIDE, 2 * width), F32),
        scratch_shapes=[pltpu.VMEM((width // LANES, mb * PAGE_SUB_PITCH, LANES), F32)],
        compiler_params=_cparams(1), name="compress_pages",
    )(pages_cm, pos_tab, w_bd)


def _assemble_kc_vc(ab, gk):
    width = 2 * N_KV_HEADS * HEAD_DIM
    kw = N_KV_HEADS * HEAD_DIM
    n = ab.shape[0]
    kcv = ab[:, :width] + pltpu.roll(ab[:, width:], n - 1, 0)
    kc = jnp.concatenate([_headnorm128(kcv[:, c * LANES:(c + 1) * LANES], gk) for c in range(kw // LANES)],
                         axis=1)
    return kc, kcv[:, kw:width]


def _kcvc_kernel(ab_ref, gk_ref, kc_ref, vc_ref):
    kc, vc = _assemble_kc_vc(ab_ref[0], gk_ref[...])
    kc_ref[0] = kc.astype(kc_ref.dtype)
    vc_ref[0] = vc.astype(vc_ref.dtype)


def _prompt_kc_vc(ab, gk):
    b, n, w = ab.shape
    kw = N_KV_HEADS * HEAD_DIM
    spec = pl.BlockSpec((1, n, kw), lambda i: (i, 0, 0))
    return pl.pallas_call(
        _kcvc_kernel, grid=(b,),
        in_specs=[pl.BlockSpec((1, n, w), lambda i: (i, 0, 0)), _const_spec((1, LANES))],
        out_specs=[spec, spec], out_shape=[jax.ShapeDtypeStruct((b, n, kw), MXU_DTYPE)] * 2,
        compiler_params=_cparams(1), name="prompt_kc_vc",
    )(ab, gk)


def _select_blocks(imp, blk, qpos, axis, n_sel):
    cur = qpos // SEL_BLOCK
    valid = blk <= cur
    forced = (blk == 0) | (blk == cur) | (blk == cur - 1)
    sc = jnp.where(forced, jnp.inf, jnp.where(valid, imp, -jnp.inf))
    cnt = jnp.zeros(sc.shape, jnp.int32)
    for j in range(n_sel):
        sj = lax.slice_in_dim(sc, j, j + 1, axis=axis)
        beats = (sj > sc) | ((sj == sc) & (blk > j))
        cnt = cnt + beats.astype(jnp.int32)
    return (cnt < min(SEL_TOPK, n_sel)) & (sc > -jnp.inf) & (blk < n_sel)


def _dot_tn(a, b):
    return lax.dot_general(a, b, (((0,), (0,)), ((), ())), preferred_element_type=F32)


def _attn_prompt_kernel(qn_ref, qr_ref, gt_ref, kc_ref, vc_ref, slc_ref, win_ref, et_ref, mt_ref, o_ref,
                        *, n_sel, tk, cs):
    tq = qn_ref.shape[0]
    t_len = slc_ref.shape[1]
    kw = N_KV_HEADS * HEAD_DIM
    rpg = 4
    ns = 2 * rpg
    qb = pl.program_id(1)
    lo = lax.broadcasted_iota(jnp.int32, (tq, LANES), 1) < HEAD_DIM
    top = lax.broadcasted_iota(jnp.int32, (LANES, tq), 0) < HEAD_DIM
    wlen = min(WINDOW + tq, t_len)

    def q_of(shape):
        return qb * tq + lax.broadcasted_iota(jnp.int32, shape, 1)

    def parts_of(ref, p):
        parts = []
        for side in range(2):
            keep = lo if side == 0 else jnp.logical_not(lo)
            for r in range(rpg):
                c = p * rpg + r
                t = ref[:, c * LANES:(c + 1) * LANES]
                parts.append(jnp.where(keep, t, jnp.zeros_like(t)))
        return parts

    def slabs(x0, x1):
        return jnp.concatenate([x0] * rpg + [x1] * rpg, axis=1)

    def flash(kv_ref, q_chunks, kt_lo, kt_hi, tkk, bias_fn):
        n_ch = len(q_chunks)

        def body(kt, carry):
            ms, ls, accs = carry
            k0 = pl.multiple_of(kt * tkk, tkk)
            bias = bias_fn(k0, tkk)
            new_m, new_l, new_acc = [], [], []
            for ci, (p, j0, qc) in enumerate(q_chunks):
                k = kv_ref[0, pl.ds(k0, tkk), p * LANES:(p + 1) * LANES]
                v = kv_ref[0, pl.ds(k0, tkk), kw + p * LANES: kw + (p + 1) * LANES]
                bs = [bias[2 * p + (j0 + jj) // rpg] for jj in range(cs)]
                s = _dot_nt(k, qc) + (jnp.concatenate(bs, axis=1) if cs > 1 else bs[0])
                m_new = jnp.maximum(ms[ci], jnp.max(s, axis=0, keepdims=True))
                alpha = jnp.exp(ms[ci] - m_new)
                pe = jnp.exp(s - m_new)
                new_m.append(m_new)
                new_l.append(alpha * ls[ci] + jnp.sum(pe, axis=0, keepdims=True))
                new_acc.append(alpha * accs[ci] + _dot_tn(v, pe.astype(MXU_DTYPE)))
            return tuple(new_m), tuple(new_l), tuple(new_acc)

        init = (tuple(jnp.full((1, cs * tq), NEG, F32) for _ in range(n_ch)),
                tuple(jnp.zeros((1, cs * tq), F32) for _ in range(n_ch)),
                tuple(jnp.zeros((LANES, cs * tq), F32) for _ in range(n_ch)))
        _, ls, accs = lax.fori_loop(kt_lo, kt_hi, body, init)
        outs = [a / jnp.maximum(l, 1e-30) for a, l in zip(accs, ls)]
        per_pair = n_ch // 2
        return [jnp.concatenate(outs[p * per_pair:(p + 1) * per_pair], axis=1) for p in range(2)]

    o_cmp, sel_bs, qr_parts = [], [], []
    for p in range(2):
        qn_st = jnp.concatenate(parts_of(qn_ref, p), axis=0)
        s = _dot_nt(kc_ref[0, :, p * LANES:(p + 1) * LANES], qn_st)
        n_cb = s.shape[0]
        cm1 = (lax.broadcasted_iota(jnp.int32, (n_cb, tq), 0) * CMP_STRIDE + (CMP_LEN - 1)) <= q_of((n_cb, tq))
        cmask = slabs(cm1, cm1)
        s = jnp.where(cmask, s, -jnp.inf)
        mx = jnp.max(s, axis=0, keepdims=True)
        mx = jnp.where(mx == -jnp.inf, 0.0, mx)
        e = jnp.where(cmask, jnp.exp(s - mx), 0.0)
        pr = e / jnp.maximum(jnp.sum(e, axis=0, keepdims=True), 1e-30)
        o_cmp.append(_dot_tn(vc_ref[0, :, p * LANES:(p + 1) * LANES], pr.astype(MXU_DTYPE)))

        for side in range(2):
            imp = pr[:, side * rpg * tq:(side * rpg + 1) * tq]
            for r in range(1, rpg):
                imp = imp + pr[:, (side * rpg + r) * tq:(side * rpg + r + 1) * tq]
            imp_t = _dot_exact_lhs(mt_ref[...], imp)[:n_sel]
            blk = lax.broadcasted_iota(jnp.int32, (n_sel, tq), 0)
            sel = _select_blocks(imp_t, blk, q_of((n_sel, tq)), 0, n_sel)
            sel_f = jnp.concatenate([jnp.where(sel, 1.0, 0.0).astype(F32),
                                     jnp.zeros((LANES - n_sel, tq), F32)], axis=0)
            sel_bs.append(sel_f.astype(MXU_DTYPE))

        qr_parts.append(parts_of(qr_ref, p))

    def slc_bias(k0, tkk):
        causal = (k0 + lax.broadcasted_iota(jnp.int32, (tkk, tq), 0)) <= q_of((tkk, tq))
        return [jnp.where((_dot(et_ref[pl.ds(k0, tkk), :], sb) > 0.5) & causal, 0.0, NEG) for sb in sel_bs]

    q_chunks = [(p, j, jnp.concatenate(qr_parts[p][j:j + cs], axis=0) if cs > 1 else qr_parts[p][j])
                for p in range(2) for j in range(0, ns, cs)]
    o_slc = flash(slc_ref, q_chunks, 0, (qb * tq + tq - 1) // tk + 1, tk, slc_bias)

    k0 = pl.multiple_of(jnp.maximum(qb * tq + tq - wlen, 0), tq)
    kpos = k0 + lax.broadcasted_iota(jnp.int32, (wlen, tq), 0)
    qp = q_of((wlen, tq))
    wb = jnp.where((kpos <= qp) & (kpos > qp - WINDOW), 0.0, NEG)
    o_win = []
    for p in range(2):
        k = win_ref[0, pl.ds(k0, wlen), p * LANES:(p + 1) * LANES]
        v = win_ref[0, pl.ds(k0, wlen), kw + p * LANES: kw + (p + 1) * LANES]
        s = _dot_nt(k, jnp.concatenate(qr_parts[p], axis=0)) + slabs(wb, wb)
        pe = jnp.exp(s - jnp.max(s, axis=0, keepdims=True))
        o_win.append(_dot_tn(v, pe.astype(MXU_DTYPE)) / jnp.maximum(jnp.sum(pe, axis=0, keepdims=True), 1e-30))

    gt = gt_ref[...]
    for p in range(2):
        for r in range(rpg):
            c = p * rpg + r
            h_lo = (2 * p) * rpg + r
            h_hi = (2 * p + 1) * rpg + r

            def gate(br):
                return jnp.where(top, gt[h_lo * N_BRANCH + br: h_lo * N_BRANCH + br + 1, :],
                                 gt[h_hi * N_BRANCH + br: h_hi * N_BRANCH + br + 1, :])

            def pick(x):
                return jnp.where(top, x[:, r * tq:(r + 1) * tq], x[:, (rpg + r) * tq:(rpg + r + 1) * tq])

            o = gate(0) * pick(o_cmp[p]) + gate(1) * pick(o_slc[p]) + gate(2) * pick(o_win[p])
            o_ref[:, c * LANES:(c + 1) * LANES] = o.T.astype(o_ref.dtype)


def _attn_prompt(qn, qr, gates_t, kc, vc, slc16, win16, e_t, mt, *, n_sel, tq=128, tk=512, cs=8):
    b, t, w = slc16.shape
    nq = t // tq
    tk = min(tk, t)
    dq = qn.shape[1]
    n_cb = kc.shape[1]
    assert tq == LANES and t % tk == 0 and tk % tq == 0
    qspec = pl.BlockSpec((tq, dq), lambda i, j: (i * nq + j, 0))
    seq = lambda shape: pl.BlockSpec(shape, lambda i, j: (i, 0, 0))
    return pl.pallas_call(
        functools.partial(_attn_prompt_kernel, n_sel=n_sel, tk=tk, cs=cs),
        grid=(b, nq),
        in_specs=[qspec, qspec, pl.BlockSpec((LANES, tq), lambda i, j: (0, i * nq + j)),
                  seq((1, n_cb, kc.shape[2])), seq((1, n_cb, vc.shape[2])),
                  seq((1, t, w)), seq((1, t, w)), _const_spec(e_t.shape), _const_spec(mt.shape)],
        out_specs=qspec, out_shape=jax.ShapeDtypeStruct((b * t, dq), MXU_DTYPE),
        compiler_params=_cparams(2), name="attn_prompt",
    )(qn, qr, gates_t, kc, vc, slc16, win16, e_t, mt)


def _attn_sample_kernel(pt_ref, qn_ref, qr_ref, gt_ref, gk_ref, rsum_ref, m_ref, e_ref, abn_ref, sn_ref,
                        wst_ref, wn_ref, *rest, n_pages, n_sel, past_len, t_new, emit_window):
    ab_pages = rest[:n_pages]
    slc_pages = rest[n_pages:2 * n_pages]
    if emit_window:
        o_ref, wout_ref, pad_s = rest[2 * n_pages:]
    else:
        (o_ref, pad_s), wout_ref = rest[2 * n_pages:], None
    kw = N_KV_HEADS * HEAD_DIM
    page = slc_pages[0].shape[2]
    nrow = qn_ref.shape[1]
    rows_per_g = nrow // N_KV_HEADS
    qn = qn_ref[0]
    qr = qr_ref[0]
    t_of_row = lax.broadcasted_iota(jnp.int32, (nrow, 1), 0) % t_new
    qpos = past_len + t_of_row

    def softmax(s, mask):
        s = jnp.where(mask, s, -jnp.inf)
        mx = jnp.max(s, axis=-1, keepdims=True)
        mx = jnp.where(mx == -jnp.inf, 0.0, mx)
        e = jnp.where(mask, jnp.exp(s - mx), 0.0)
        return e / jnp.maximum(jnp.sum(e, axis=-1, keepdims=True), 1e-30)

    n_ab = m_ref.shape[0]
    pieces = [r[0] for r in ab_pages] + [abn_ref[0]]
    n_have = sum(x.shape[0] for x in pieces)
    pieces.append(jnp.zeros((n_ab - n_have, pieces[0].shape[1]), F32))
    kc, vc = _assemble_kc_vc(jnp.concatenate(pieces, axis=0), gk_ref[...])
    blk_c = lax.broadcasted_iota(jnp.int32, (nrow, n_ab), 1)
    cmask = (blk_c * CMP_STRIDE + (CMP_LEN - 1)) <= qpos
    pc = softmax(_dot_nt(qn, kc.astype(MXU_DTYPE)), cmask)
    o_cmp = _dot(pc.astype(MXU_DTYPE), vc.astype(MXU_DTYPE))

    imp = _dot_exact_lhs(rsum_ref[...], pc)
    imp_s = _dot_exact_rhs(imp, m_ref[...])
    blk = lax.broadcasted_iota(jnp.int32, imp_s.shape, 1)
    sel = _select_blocks(imp_s, blk, jnp.broadcast_to(qpos, imp_s.shape), 1, n_sel)
    al = _dot(jnp.where(sel, 1.0, 0.0).astype(MXU_DTYPE), e_ref[...])
    n_keys = al.shape[1]
    kpos = lax.broadcasted_iota(jnp.int32, (nrow, n_keys), 1)
    smask = (al > 0.5) & (kpos <= qpos)

    pad_s[...] = jnp.zeros(pad_s.shape, F32)
    pad_s[0:sn_ref.shape[1], :] = sn_ref[0]
    new_rows = pad_s[...]
    s = jnp.concatenate([_dot(qr, r[0, :kw, :].astype(MXU_DTYPE)) for r in slc_pages]
                        + [_dot_nt(qr, new_rows[:, :kw].astype(MXU_DTYPE))], axis=1)
    ps = softmax(s, smask).astype(MXU_DTYPE)
    o_slc = _dot(ps[:, n_pages * page:], new_rows[:, kw:].astype(MXU_DTYPE))
    for i, r in enumerate(slc_pages):
        o_slc = o_slc + _dot_nt(ps[:, i * page:(i + 1) * page], r[0, kw:, :].astype(MXU_DTYPE))

    wbuf = wst_ref.shape[2]
    pad_s[...] = jnp.zeros(pad_s.shape, F32)
    pad_s[0:wn_ref.shape[1], :] = wn_ref[0]
    new_rows = pad_s[...]
    s = jnp.concatenate([_dot(qr, wst_ref[0, :kw, :].astype(MXU_DTYPE)),
                         _dot_nt(qr, new_rows[:, :kw].astype(MXU_DTYPE))], axis=1)
    wpos = (past_len - wbuf) + lax.broadcasted_iota(jnp.int32, s.shape, 1)
    wmask = (wpos <= qpos) & (wpos > qpos - WINDOW) & (wpos >= 0)
    pw = softmax(s, wmask).astype(MXU_DTYPE)
    o_win = (_dot_nt(pw[:, :wbuf], wst_ref[0, kw:, :].astype(MXU_DTYPE))
             + _dot(pw[:, wbuf:], new_rows[:, kw:].astype(MXU_DTYPE)))

    gt = gt_ref[0]
    o = gt[:, 0:1] * o_cmp + gt[:, 1:2] * o_slc + gt[:, 2:3] * o_win
    lane_g = lax.broadcasted_iota(jnp.int32, (rows_per_g, kw), 1) // HEAD_DIM
    acc = jnp.zeros((rows_per_g, kw), F32)
    for g in range(N_KV_HEADS):
        acc = acc + jnp.where(lane_g == g, o[g * rows_per_g:(g + 1) * rows_per_g], 0.0)
    o_ref[0] = acc

    if wout_ref is not None:
        rolled = pltpu.roll(wst_ref[0], wbuf - t_new, 1)
        wout_ref[0, :, 0:wbuf - LANES] = rolled[:, 0:wbuf - LANES]
        new_t = pltpu.roll(new_rows.T, LANES - t_new, 1)
        ln = lax.broadcasted_iota(jnp.int32, new_t.shape, 1)
        wout_ref[0, :, wbuf - LANES:wbuf] = jnp.where(ln >= LANES - t_new, new_t, rolled[:, wbuf - LANES:wbuf])


def _attn_sample(page_table, qn, qr, gates, gk, rsum, m_mat, e_mat, ab_pool, ab_new, cache_slc, slc_new,
                 win_state, win_new, *, n_sel, past_len, t_new, emit_window):
    nb, n_pages = page_table.shape
    nrow, kw = qn.shape[1], qn.shape[2]
    width = cache_slc.shape[1]
    page = cache_slc.shape[2]
    wbuf = win_state.shape[2]
    per_b = lambda shape: pl.BlockSpec(shape, lambda i, pt: (i, 0, 0))

    def page_spec(shape, p):
        return pl.BlockSpec(shape, lambda i, pt: (pt[i * n_pages + p], 0, 0))

    in_specs = [per_b((1, nrow, kw)), per_b((1, nrow, kw)), per_b((1, nrow, N_BRANCH)),
                pl.BlockSpec((1, LANES), lambda i, pt: (0, 0)),
                pl.BlockSpec(rsum.shape, lambda i, pt: (0, 0)),
                pl.BlockSpec(m_mat.shape, lambda i, pt: (0, 0)),
                pl.BlockSpec(e_mat.shape, lambda i, pt: (0, 0)),
                per_b((1,) + ab_new.shape[1:]), per_b((1,) + slc_new.shape[1:]),
                per_b((1, width, wbuf)), per_b((1,) + win_new.shape[1:])]
    in_specs += [page_spec((1,) + ab_pool.shape[1:], p) for p in range(n_pages)]
    in_specs += [page_spec((1, width, page), p) for p in range(n_pages)]
    out_specs = [per_b((1, nrow // N_KV_HEADS, kw))]
    out_shape = [jax.ShapeDtypeStruct((nb, nrow // N_KV_HEADS, kw), F32)]
    if emit_window:
        out_specs.append(per_b((1, width, wbuf)))
        out_shape.append(jax.ShapeDtypeStruct((nb, width, wbuf), F32))
    grid_spec = pltpu.PrefetchScalarGridSpec(
        num_scalar_prefetch=1, grid=(nb,), in_specs=in_specs, out_specs=out_specs,
        scratch_shapes=[pltpu.VMEM((page, width), F32)])
    res = pl.pallas_call(
        functools.partial(_attn_sample_kernel, n_pages=n_pages, n_sel=n_sel, past_len=past_len, t_new=t_new,
                          emit_window=emit_window),
        grid_spec=grid_spec, out_shape=out_shape,
        compiler_params=_cparams(1), name="attn_sample",
    )(page_table.reshape(-1), qn, qr, gates, gk, rsum, m_mat, e_mat, ab_new, slc_new, win_state, win_new,
      *([ab_pool] * n_pages), *([cache_slc] * n_pages))
    return (res[0], res[1]) if emit_window else (res[0], None)


def _rope_tables(pos):
    half = ROPE_DIM // 2
    inv_freq = ROPE_THETA ** (-jnp.arange(half, dtype=F32) / half)
    ang = pos.astype(F32)[:, None] * inv_freq[None, :]
    cos, sin = jnp.cos(ang), jnp.sin(ang)
    n = pos.shape[0]
    z = jnp.zeros((n, HEAD_DIM - ROPE_DIM), F32)
    zh = jnp.zeros((n, half), F32)
    c_head = jnp.concatenate([cos, cos, z + 1.0], axis=1)
    dn_head = jnp.concatenate([zh, sin, z], axis=1)
    up_head = jnp.concatenate([-sin, zh, z], axis=1)
    tile2 = lambda a: jnp.concatenate([a, a], axis=1)
    return tile2(c_head), tile2(dn_head), tile2(up_head)


def _sel_matrices(n_cmp, n_cmp_pad, n_sel, n_keys):
    i = jnp.arange(n_cmp_pad)[:, None]
    j = jnp.arange(LANES)[None, :]
    m = ((i * CMP_STRIDE < j * SEL_BLOCK + SEL_BLOCK) & (i * CMP_STRIDE + CMP_LEN > j * SEL_BLOCK)
         & (i < n_cmp) & (j < n_sel))
    e = (jnp.arange(n_keys)[None, :] // SEL_BLOCK == jnp.arange(LANES)[:, None]) & (jnp.arange(LANES)[:, None] < n_sel)
    return m.astype(MXU_DTYPE), e.astype(MXU_DTYPE)


def _tile2(g):
    return jnp.concatenate([g, g]).reshape(1, LANES).astype(F32)


def kernel(x_prompt, x_sample, cache_kv_cmp, cache_kv_slc, state_kv_win, page_table, norm_mix, norm_ffn,
           ffn_w_gate, ffn_w_up, ffn_w_down, a_w_in, a_ln_g, a_ln_b, a_w_spatial, a_b_spatial, a_w_out,
           kv_norm, kv_w, k_norm_cmp, k_norm_slc, k_norm_win, cmp_pos_k, cmp_pos_v, cmp_w_k, cmp_w_v,
           b_w_in, b_q_norm, b_w_out):
    bf = MXU_DTYPE
    depth = norm_mix.shape[0]
    n_a = a_w_in.shape[0]
    bsz, t_p, d_model = x_prompt.shape
    nb, t_s, _ = x_sample.shape
    n_pages = page_table.shape[1]
    page = cache_kv_cmp.shape[1]
    past_len = n_pages * page
    width = 2 * N_KV_HEADS * HEAD_DIM
    kw = N_KV_HEADS * HEAD_DIM
    n_heads = b_w_out.shape[1] // HEAD_DIM
    rpg = n_heads // N_KV_HEADS
    dq = n_heads * HEAD_DIM
    assert N_KV_HEADS == 4 and rpg == 4 and t_p % CHUNK == 0 and CHUNK % t_s == 0 and t_s <= 8

    wg, wu, wd = ffn_w_gate.astype(bf), ffn_w_up.astype(bf), ffn_w_down.astype(bf)
    a_in, a_out = a_w_in.astype(bf), a_w_out.astype(bf)
    tril = jnp.tril(jnp.ones((CHUNK, CHUNK), bool))
    ws_p = jnp.where(tril[None, None], a_w_spatial, 0.0).astype(bf)
    bs_p = jnp.swapaxes(a_b_spatial, 1, 2)
    rep = CHUNK // t_s
    ws4 = jnp.where(tril[None, None, :t_s, :t_s], a_w_spatial[:, :, :t_s, :t_s], 0.0)
    ws_s = jnp.einsum('ab,lgts->lgatbs', jnp.eye(rep, dtype=F32), ws4).reshape(n_a, A_GROUPS, CHUNK, CHUNK).astype(bf)
    bs_s = jnp.tile(jnp.swapaxes(a_b_spatial[:, :, :t_s], 1, 2), (1, rep, 1))
    perm = jnp.array([(2 * p + side) * rpg + r for p in range(2) for r in range(rpg) for side in range(2)])
    wq = b_w_in[:, :, :dq].reshape(-1, d_model, n_heads, HEAD_DIM)[:, :, perm].reshape(-1, d_model, dq).astype(bf)
    wgate = jnp.pad(b_w_in[:, :, dq:], ((0, 0), (0, 0), (0, LANES - N_BRANCH * n_heads))).astype(bf)
    wo_pair = b_w_out.reshape(-1, n_heads, HEAD_DIM, d_model)[:, perm].reshape(-1, dq, d_model).astype(bf)
    wo_flat = b_w_out.astype(bf)
    kvw = kv_w.astype(bf)
    eye_g = jnp.eye(N_KV_HEADS, dtype=F32)
    w_bd = jnp.stack([jnp.einsum('ab,lde->ladbe', eye_g, w).reshape(CMP_LEN, kw, kw)
                      for w in (cmp_w_k, cmp_w_v)]).astype(bf)
    pos_tab = jnp.concatenate([jnp.tile(cmp_pos_k, (1, N_KV_HEADS)), jnp.tile(cmp_pos_v, (1, N_KV_HEADS))], axis=1)
    gk_cmp, gk_slc, gk_win = _tile2(k_norm_cmp), _tile2(k_norm_slc), _tile2(k_norm_win)

    def dense_a_layers(h, ws_eff, bs_eff, emit_v):
        vs = []
        for l in range(n_a):
            h, v = _gmlp_layer(h, norm_mix[l], a_in[l], a_ln_g[l], a_ln_b[l], ws_eff[l], bs_eff[l], a_out[l],
                               emit_v=emit_v)
            vs.append(v)
            h = _ffn_layer(h, norm_ffn[l], wg[l], wu[l], wd[l])
        return h, vs

    tabs_p = _rope_tables(jnp.arange(t_p))
    h = x_prompt.reshape(bsz * t_p, d_model)
    h, _ = dense_a_layers(h, ws_p, bs_p, False)
    kv_cmp_p, kv_slc_p, kv_win_p, cmp_slabs, slc16, win16 = _kv_project(h, kv_norm, kvw, gk_slc, gk_win, tabs_p,
                                                                        seq_len=t_p)
    ab = _compress_partial(cmp_slabs, pos_tab, w_bd)
    n_sub = t_p // CMP_STRIDE
    kc, vc = _prompt_kc_vc(ab.reshape(bsz, n_sub, 2 * width), gk_cmp)
    n_sel_p = t_p // SEL_BLOCK
    m_p, e_p = _sel_matrices(n_sub - 1, n_sub, n_sel_p, t_p)
    for j in range(depth - n_a):
        l = n_a + j
        qn, qr, gates = _q_project(h, norm_mix[l], wq[j], wgate[j], _tile2(b_q_norm[j]) * ATTN_SCALE, tabs_p)
        o = _attn_prompt(qn, qr, gates, kc, vc, slc16.reshape(bsz, t_p, width), win16.reshape(bsz, t_p, width),
                         e_p.T, m_p.T, n_sel=n_sel_p)
        h = _ffn_layer(h, norm_ffn[l], wg[l], wu[l], wd[l], attn=o, w_o=wo_pair[j])
    y_prompt = h.reshape(bsz, t_p, d_model)
    row5 = lambda a, b_, t_: a.reshape(b_, t_, 2, N_KV_HEADS, HEAD_DIM)
    from_cm = lambda a: jnp.transpose(a.reshape(a.shape[0], 2, N_KV_HEADS, HEAD_DIM, a.shape[2]), (0, 4, 1, 2, 3))
    to_cm = lambda a: jnp.transpose(a, (0, 2, 3, 4, 1)).reshape(a.shape[0], width, a.shape[1])
    w_keep = min(WINDOW, t_p)
    kv_win_prompt = from_cm(kv_win_p[:, :, t_p - w_keep:])

    m_s = nb * t_s
    tabs_s = _rope_tables(past_len + (jnp.arange(nb * t_s) % t_s))
    hs = x_sample.reshape(m_s, d_model)
    hs, vs = dense_a_layers(hs, ws_s, bs_s, True)
    a_v_sample = jnp.stack(vs).reshape(n_a, nb, t_s, -1)
    kv_cmp_s, kv_slc_s, kv_win_s, cmp_slabs_s = _kv_project(hs, kv_norm, kvw, gk_slc, gk_win, tabs_s)
    n_pool = cache_kv_cmp.shape[0]
    sub_pp = page // CMP_STRIDE
    ab_pool = _compress_pages(to_cm(cache_kv_cmp), pos_tab, w_bd).reshape(n_pool, sub_pp, 2 * width)
    t_full = -(-(past_len + t_s) // SEL_BLOCK) * SEL_BLOCK
    n_new_sub = (t_full - past_len) // CMP_STRIDE
    new_cmp = jnp.pad(cmp_slabs_s.reshape(-1, nb, t_s, LANES), ((0, 0), (0, 0), (0, t_full - past_len - t_s), (0, 0)))
    ab_new = _compress_partial(new_cmp.reshape(-1, nb * (t_full - past_len), LANES), pos_tab, w_bd)
    ab_new = jnp.pad(ab_new.reshape(nb, n_new_sub, 2 * width), ((0, 0), (0, 8 - n_new_sub), (0, 0)))
    n_cmp_s = t_full // CMP_STRIDE - 1
    n_sel_s = t_full // SEL_BLOCK
    n_cmp_pad = 256
    n_keys = past_len + page
    m_sm, e_sm = _sel_matrices(n_cmp_s, n_cmp_pad, n_sel_s, n_keys)
    nrow = n_heads * t_s
    ridx = jnp.arange(nrow)
    rsum = ((ridx[:, None] // (rpg * t_s) == ridx[None, :] // (rpg * t_s))
            & (ridx[:, None] % t_s == ridx[None, :] % t_s)).astype(bf)
    pad8 = lambda a: jnp.pad(a.reshape(nb, t_s, width), ((0, 0), (0, 8 - t_s), (0, 0)))
    slc_new, win_new = pad8(kv_slc_s), pad8(kv_win_s)
    inv_perm = jnp.argsort(perm)

    def pad_q(q):
        q = q.reshape(nb, t_s, n_heads, HEAD_DIM)[:, :, inv_perm].reshape(nb, t_s, N_KV_HEADS, rpg, HEAD_DIM)
        q = jnp.transpose(q, (0, 2, 3, 1, 4))
        q = jnp.einsum('bgrtd,gk->bgrtkd', q, jnp.eye(N_KV_HEADS, dtype=q.dtype))
        return q.reshape(nb, nrow, kw)

    slc_cm, win_cm = to_cm(cache_kv_slc), to_cm(state_kv_win)
    kv_win_sample = None
    for j in range(depth - n_a):
        l = n_a + j
        qn, qr, gates = _q_project(hs, norm_mix[l], wq[j], wgate[j], _tile2(b_q_norm[j]) * ATTN_SCALE, tabs_s)
        g3 = gates[:N_BRANCH * n_heads].T.reshape(nb, t_s, N_KV_HEADS, rpg, N_BRANCH)
        g3 = jnp.transpose(g3, (0, 2, 3, 1, 4)).reshape(nb, nrow, N_BRANCH)
        o, w_out = _attn_sample(page_table, pad_q(qn), pad_q(qr), g3, gk_cmp, rsum, m_sm, e_sm,
                                ab_pool, ab_new, slc_cm, slc_new, win_cm, win_new,
                                n_sel=n_sel_s, past_len=past_len, t_new=t_s, emit_window=j == 0)
        if j == 0:
            kv_win_sample = from_cm(w_out)
        o = jnp.transpose(o.reshape(nb, rpg, t_s, N_KV_HEADS, HEAD_DIM), (0, 2, 3, 1, 4)).reshape(m_s, dq)
        hs = _ffn_layer(hs, norm_ffn[l], wg[l], wu[l], wd[l], attn=o.astype(bf), w_o=wo_flat[j])
    y_sample = hs.reshape(nb, t_s, d_model)

    return (y_prompt, y_sample, from_cm(kv_cmp_p), row5(kv_cmp_s, nb, t_s),
            from_cm(kv_slc_p), row5(kv_slc_s, nb, t_s), kv_win_prompt, kv_win_sample, a_v_sample)
```

```python
import functools

import jax
import jax.numpy as jnp
from jax import lax
from jax.experimental import pallas as pl
from jax.experimental.pallas import tpu as pltpu

CHUNK = 128
A_GROUPS = 8
HEAD_DIM = 64
N_KV_HEADS = 4
ROPE_DIM = HEAD_DIM // 4
ROPE_THETA = 500000.0
CMP_STRIDE = 16
CMP_LEN = 2 * CMP_STRIDE
SEL_BLOCK = 64
SEL_TOPK = 16
WINDOW = 512
N_BRANCH = 3
EPS = 1e-6
ATTN_SCALE = HEAD_DIM ** -0.5

LANES = 128
PAGE_SUB_PITCH = 24
VMEM_LIMIT_BYTES = 56 * 1024 * 1024
MXU_DTYPE = jnp.bfloat16
NEG = -1e30
F32 = jnp.float32


def _cparams(n_grid):
    return pltpu.CompilerParams(dimension_semantics=("arbitrary",) * n_grid,
                                vmem_limit_bytes=VMEM_LIMIT_BYTES)


def _const_spec(shape):
    nd = len(shape)
    return pl.BlockSpec(shape, lambda *_: (0,) * nd, pipeline_mode=pl.Buffered(1))


def _dot(a, b):
    return jnp.dot(a, b, preferred_element_type=F32)


def _dot_nt(a, b):
    return lax.dot_general(a, b, (((1,), (1,)), ((), ())), preferred_element_type=F32)


def _split3(x):
    hi = x.astype(MXU_DTYPE)
    r1 = x - hi.astype(F32)
    mid = r1.astype(MXU_DTYPE)
    lo = (r1 - mid.astype(F32)).astype(MXU_DTYPE)
    return hi, mid, lo


def _dot_exact_lhs(a, x):
    hi, mid, lo = _split3(x)
    return _dot(a, hi) + _dot(a, mid) + _dot(a, lo)


def _dot_exact_rhs(x, b):
    hi, mid, lo = _split3(x)
    return _dot(hi, b) + _dot(mid, b) + _dot(lo, b)


def _rms(x, g):
    return x * lax.rsqrt(jnp.mean(x * x, axis=-1, keepdims=True) + EPS) * g


def _headnorm128(x, g):
    lo = lax.broadcasted_iota(jnp.int32, x.shape, 1) < HEAD_DIM
    sq = x * x
    s_lo = jnp.sum(jnp.where(lo, sq, 0.0), axis=-1, keepdims=True)
    s_hi = jnp.sum(jnp.where(lo, 0.0, sq), axis=-1, keepdims=True)
    ms = jnp.where(lo, s_lo, s_hi) * (1.0 / HEAD_DIM)
    return x * lax.rsqrt(ms + EPS) * g


def _rope128(x, cos, s_dn, s_up):
    half = ROPE_DIM // 2
    return x * cos + pltpu.roll(x, half, 1) * s_dn + pltpu.roll(x, LANES - half, 1) * s_up


def _gmlp_kernel(h_ref, g_ref, win_ref, lng_ref, lnb_ref, ws_ref, bs_ref, wout_ref, *rest, emit_v):
    if emit_v:
        o_ref, v_ref, gated = rest
    else:
        o_ref, gated = rest
    tm = h_ref.shape[0]
    d_a = lng_ref.shape[1]
    dg = d_a // A_GROUPS
    x = h_ref[...]
    xn = _rms(x, g_ref[...]).astype(MXU_DTYPE)
    u = jax.nn.gelu(_dot(xn, win_ref[:, :d_a]))
    v = jax.nn.gelu(_dot(xn, win_ref[:, d_a:]))
    vc = v - jnp.mean(v, axis=-1, keepdims=True)
    v = vc * lax.rsqrt(jnp.mean(vc * vc, axis=-1, keepdims=True) + EPS) * lng_ref[...] + lnb_ref[...]
    if emit_v:
        v_ref[...] = v
    vb = v.astype(MXU_DTYPE)
    for c in range(tm // CHUNK):
        r0, r1 = c * CHUNK, (c + 1) * CHUNK
        for g in range(A_GROUPS):
            sv = _dot(ws_ref[g], vb[r0:r1, g * dg:(g + 1) * dg]) + bs_ref[:, g:g + 1]
            gated[r0:r1, g * dg:(g + 1) * dg] = (u[r0:r1, g * dg:(g + 1) * dg] * sv).astype(MXU_DTYPE)
    o_ref[...] = x + _dot(gated[...], wout_ref[...])


def _gmlp_layer(h, g, w_in, ln_g, ln_b, ws_eff, bs_eff, w_out, *, emit_v, tm=512):
    m, d = h.shape
    tm = min(tm, m)
    d_a = ln_g.shape[-1]
    row = pl.BlockSpec((tm, d), lambda i: (i, 0))
    out_shape = [jax.ShapeDtypeStruct((m, d), F32)]
    out_specs = [row]
    if emit_v:
        out_shape.append(jax.ShapeDtypeStruct((m, d_a), F32))
        out_specs.append(pl.BlockSpec((tm, d_a), lambda i: (i, 0)))
    res = pl.pallas_call(
        functools.partial(_gmlp_kernel, emit_v=emit_v),
        grid=(m // tm,),
        in_specs=[row, _const_spec((1, d)), _const_spec(w_in.shape), _const_spec((1, d_a)),
                  _const_spec((1, d_a)), _const_spec(ws_eff.shape), _const_spec(bs_eff.shape),
                  _const_spec(w_out.shape)],
        out_specs=out_specs, out_shape=out_shape,
        scratch_shapes=[pltpu.VMEM((tm, d_a), MXU_DTYPE)],
        compiler_params=_cparams(1), name="gmlp_layer",
    )(h, g.reshape(1, d), w_in, ln_g.reshape(1, d_a), ln_b.reshape(1, d_a), ws_eff, bs_eff, w_out)
    return (res[0], res[1]) if emit_v else (res[0], None)


def _ffn_kernel(*refs, with_proj):
    if with_proj:
        h_ref, a_ref, wo_ref, g_ref, wg_ref, wu_ref, wd_ref, o_ref = refs
        x = h_ref[...] + _dot(a_ref[...], wo_ref[...])
    else:
        h_ref, g_ref, wg_ref, wu_ref, wd_ref, o_ref = refs
        x = h_ref[...]
    xn = _rms(x, g_ref[...]).astype(MXU_DTYPE)
    mid = (jax.nn.silu(_dot(xn, wg_ref[...])) * _dot(xn, wu_ref[...])).astype(MXU_DTYPE)
    o_ref[...] = x + _dot(mid, wd_ref[...])


def _ffn_layer(h, g, wg, wu, wd, attn=None, w_o=None, *, tm=512):
    m, d = h.shape
    tm = min(tm, m)
    row = pl.BlockSpec((tm, d), lambda i: (i, 0))
    with_proj = attn is not None
    args, specs = [h], [row]
    if with_proj:
        args += [attn, w_o]
        specs += [pl.BlockSpec((tm, attn.shape[1]), lambda i: (i, 0)), _const_spec(w_o.shape)]
    args += [g.reshape(1, d), wg, wu, wd]
    specs += [_const_spec((1, d)), _const_spec(wg.shape), _const_spec(wu.shape), _const_spec(wd.shape)]
    return pl.pallas_call(
        functools.partial(_ffn_kernel, with_proj=with_proj),
        grid=(m // tm,), in_specs=specs, out_specs=row,
        out_shape=jax.ShapeDtypeStruct((m, d), F32),
        compiler_params=_cparams(1), name="ffn_layer",
    )(*args)


def _kvproj_kernel(h_ref, g_ref, w_ref, gs_ref, gw_ref, cos_ref, sdn_ref, sup_ref,
                   cmp_ref, slc_ref, win_ref, slabs_ref, *bf_refs, channel_major):
    x = h_ref[...]
    xn = _rms(x, g_ref[...]).astype(MXU_DTYPE)
    kv = _dot(xn, w_ref[...])
    width = 2 * N_KV_HEADS * HEAD_DIM
    kw = N_KV_HEADS * HEAD_DIM
    cos, sdn, sup = cos_ref[...], sdn_ref[...], sup_ref[...]

    def emit(o_ref, c, tile):
        if channel_major:
            o_ref[0, c * LANES:(c + 1) * LANES, :] = tile.T
        else:
            o_ref[:, c * LANES:(c + 1) * LANES] = tile

    for c in range(width // LANES):
        tile = kv[:, c * LANES:(c + 1) * LANES]
        emit(cmp_ref, c, tile)
        slabs_ref[c] = tile
    for bi, (base, gk_ref, o_ref) in enumerate(((width, gs_ref, slc_ref), (2 * width, gw_ref, win_ref))):
        for c in range(width // LANES):
            tile = kv[:, base + c * LANES: base + (c + 1) * LANES]
            if c < kw // LANES:
                tile = _rope128(_headnorm128(tile, gk_ref[...]), cos, sdn, sup)
            emit(o_ref, c, tile)
            if channel_major:
                bf_refs[bi][:, c * LANES:(c + 1) * LANES] = tile.astype(MXU_DTYPE)


def _kv_project(h, kv_norm, kv_w, gk_slc, gk_win, tabs, *, seq_len=None, tm=512):
    m, d = h.shape
    tm = min(tm, m)
    width = 2 * N_KV_HEADS * HEAD_DIM
    n_tab = tabs[0].shape[0] // tm
    row = pl.BlockSpec((tm, d), lambda i: (i, 0))
    tab = pl.BlockSpec((tm, LANES), lambda i: (i % n_tab, 0))
    orow = pl.BlockSpec((tm, width), lambda i: (i, 0))
    slab = pl.BlockSpec((width // LANES, tm, LANES), lambda i: (0, i, 0))
    slab_shape = jax.ShapeDtypeStruct((width // LANES, m, LANES), F32)
    if seq_len is not None:
        nblk = seq_len // tm
        ocm = pl.BlockSpec((1, width, tm), lambda i: (i // nblk, 0, i % nblk))
        out_specs = [ocm] * 3 + [slab, orow, orow]
        out_shape = ([jax.ShapeDtypeStruct((m // seq_len, width, seq_len), F32)] * 3 + [slab_shape]
                     + [jax.ShapeDtypeStruct((m, width), MXU_DTYPE)] * 2)
    else:
        out_specs = [orow] * 3 + [slab]
        out_shape = [jax.ShapeDtypeStruct((m, width), F32)] * 3 + [slab_shape]
    return pl.pallas_call(
        functools.partial(_kvproj_kernel, channel_major=seq_len is not None), grid=(m // tm,),
        in_specs=[row, _const_spec((1, d)), _const_spec(kv_w.shape), _const_spec((1, LANES)),
                  _const_spec((1, LANES)), tab, tab, tab],
        out_specs=out_specs, out_shape=out_shape,
        compiler_params=_cparams(1), name="kv_project",
    )(h, kv_norm.reshape(1, d), kv_w, gk_slc, gk_win, *tabs)


def _qproj_kernel(h_ref, g_ref, wq_ref, wgt_ref, gq_ref, cos_ref, sdn_ref, sup_ref,
                  qn_ref, qr_ref, gates_ref):
    x = h_ref[...]
    xn = _rms(x, g_ref[...]).astype(MXU_DTYPE)
    q = _dot(xn, wq_ref[...])
    gates_ref[...] = jax.nn.sigmoid(_dot(xn, wgt_ref[...])).T
    cos, sdn, sup = cos_ref[...], sdn_ref[...], sup_ref[...]
    for c in range(q.shape[1] // LANES):
        qn = _headnorm128(q[:, c * LANES:(c + 1) * LANES], gq_ref[...])
        qn_ref[:, c * LANES:(c + 1) * LANES] = qn.astype(qn_ref.dtype)
        qr_ref[:, c * LANES:(c + 1) * LANES] = _rope128(qn, cos, sdn, sup).astype(qr_ref.dtype)


def _q_project(h, g, w_q, w_gate, gq, tabs, *, tm=512):
    m, d = h.shape
    tm = min(tm, m)
    dq = w_q.shape[1]
    n_tab = tabs[0].shape[0] // tm
    row = pl.BlockSpec((tm, d), lambda i: (i, 0))
    tab = pl.BlockSpec((tm, LANES), lambda i: (i % n_tab, 0))
    qrow = pl.BlockSpec((tm, dq), lambda i: (i, 0))
    return pl.pallas_call(
        _qproj_kernel, grid=(m // tm,),
        in_specs=[row, _const_spec((1, d)), _const_spec(w_q.shape), _const_spec(w_gate.shape),
                  _const_spec((1, LANES)), tab, tab, tab],
        out_specs=[qrow, qrow, pl.BlockSpec((LANES, tm), lambda i: (0, i))],
        out_shape=[jax.ShapeDtypeStruct((m, dq), MXU_DTYPE)] * 2 + [jax.ShapeDtypeStruct((LANES, m), F32)],
        compiler_params=_cparams(1), name="q_project",
    )(h, g.reshape(1, d), w_q, w_gate, gq, *tabs)


def _compress_from_slabs(x_ref, pos_ref, w_ref, o_ref, pitch=CMP_STRIDE):
    width = 2 * N_KV_HEADS * HEAD_DIM
    kw = N_KV_HEADS * HEAD_DIM
    mb = o_ref.shape[0]
    spk = kw // LANES
    for kv in range(2):
        acc_a = jnp.zeros((mb, kw), F32)
        acc_b = jnp.zeros((mb, kw), F32)
        for l in range(CMP_STRIDE):
            z = jnp.concatenate([x_ref[kv * spk + j, pl.ds(l, mb, stride=pitch), :] for j in range(spk)], axis=1)
            pa = pos_ref[l:l + 1, kv * kw:(kv + 1) * kw]
            pb = pos_ref[CMP_STRIDE + l:CMP_STRIDE + l + 1, kv * kw:(kv + 1) * kw]
            acc_a = acc_a + _dot((z + pa).astype(MXU_DTYPE), w_ref[kv, l])
            acc_b = acc_b + _dot((z + pb).astype(MXU_DTYPE), w_ref[kv, CMP_STRIDE + l])
        o_ref[:, kv * kw:(kv + 1) * kw] = acc_a
        o_ref[:, width + kv * kw: width + (kv + 1) * kw] = acc_b


def _compress_pages_kernel(x_ref, pos_ref, w_ref, o_ref, xt):
    n_pg, width, page = x_ref.shape
    sub_pp = page // CMP_STRIDE

    def body(pg, carry):
        r0 = pl.multiple_of(pg * (sub_pp * PAGE_SUB_PITCH), 8)
        for c in range(width // LANES):
            rows = x_ref[pg, c * LANES:(c + 1) * LANES, :].T
            for m in range(sub_pp):
                xt[c, pl.ds(r0 + m * PAGE_SUB_PITCH, CMP_STRIDE), :] = rows[m * CMP_STRIDE:(m + 1) * CMP_STRIDE]
        return carry

    lax.fori_loop(0, n_pg, body, 0, unroll=4)
    _compress_from_slabs(xt, pos_ref, w_ref, o_ref, pitch=PAGE_SUB_PITCH)


def _compress_partial(slabs, pos_tab, w_bd, *, mb=256):
    n_slab, n_rows, _ = slabs.shape
    n_sub = n_rows // CMP_STRIDE
    mb = min(mb, n_sub)
    width = 2 * N_KV_HEADS * HEAD_DIM
    return pl.pallas_call(
        _compress_from_slabs, grid=(n_sub // mb,),
        in_specs=[pl.BlockSpec((n_slab, mb * CMP_STRIDE, LANES), lambda i: (0, i, 0)),
                  _const_spec(pos_tab.shape), _const_spec(w_bd.shape)],
        out_specs=pl.BlockSpec((mb, 2 * width), lambda i: (i, 0)),
        out_shape=jax.ShapeDtypeStruct((n_sub, 2 * width), F32),
        compiler_params=_cparams(1), name="compress_partial",
    )(slabs, pos_tab, w_bd)


def _compress_pages(pages_cm, pos_tab, w_bd, *, pages_per_step=32):
    n_pool, width, page = pages_cm.shape
    pps = min(pages_per_step, n_pool)
    assert n_pool % pps == 0
    mb = pps * page // CMP_STRIDE
    return pl.pallas_call(
        _compress_pages_kernel, grid=(n_pool // pps,),
        in_specs=[pl.BlockSpec((pps, width, page), lambda i: (i, 0, 0)),
                  _const_spec(pos_tab.shape), _const_spec(w_bd.shape)],
        out_specs=pl.BlockSpec((mb, 2 * width), lambda i: (i, 0)),
        out_shape=jax.ShapeDtypeStruct((n_pool * page // CMP_STRIDE, 2 * width), F32),
        scratch_shapes=[pltpu.VMEM((width // LANES, mb * PAGE_SUB_PITCH, LANES), F32)],
        compiler_params=_cparams(1), name="compress_pages",
    )(pages_cm, pos_tab, w_bd)


def _assemble_kc_vc(ab, gk):
    width = 2 * N_KV_HEADS * HEAD_DIM
    kw = N_KV_HEADS * HEAD_DIM
    n = ab.shape[0]
    kcv = ab[:, :width] + pltpu.roll(ab[:, width:], n - 1, 0)
    kc = jnp.concatenate([_headnorm128(kcv[:, c * LANES:(c + 1) * LANES], gk) for c in range(kw // LANES)],
                         axis=1)
    return kc, kcv[:, kw:width]


def _kcvc_kernel(ab_ref, gk_ref, kc_ref, vc_ref):
    kc, vc = _assemble_kc_vc(ab_ref[0], gk_ref[...])
    kc_ref[0] = kc.astype(kc_ref.dtype)
    vc_ref[0] = vc.astype(vc_ref.dtype)


def _prompt_kc_vc(ab, gk):
    b, n, w = ab.shape
    kw = N_KV_HEADS * HEAD_DIM
    spec = pl.BlockSpec((1, n, kw), lambda i: (i, 0, 0))
    return pl.pallas_call(
        _kcvc_kernel, grid=(b,),
        in_specs=[pl.BlockSpec((1, n, w), lambda i: (i, 0, 0)), _const_spec((1, LANES))],
        out_specs=[spec, spec], out_shape=[jax.ShapeDtypeStruct((b, n, kw), MXU_DTYPE)] * 2,
        compiler_params=_cparams(1), name="prompt_kc_vc",
    )(ab, gk)


def _select_blocks(imp, blk, qpos, axis, n_sel):
    cur = qpos // SEL_BLOCK
    valid = blk <= cur
    forced = (blk == 0) | (blk == cur) | (blk == cur - 1)
    sc = jnp.where(forced, jnp.inf, jnp.where(valid, imp, -jnp.inf))
    cnt = jnp.zeros(sc.shape, jnp.int32)
    for j in range(n_sel):
        sj = lax.slice_in_dim(sc, j, j + 1, axis=axis)
        beats = (sj > sc) | ((sj == sc) & (blk > j))
        cnt = cnt + beats.astype(jnp.int32)
    return (cnt < min(SEL_TOPK, n_sel)) & (sc > -jnp.inf) & (blk < n_sel)


def _dot_tn(a, b):
    return lax.dot_general(a, b, (((0,), (0,)), ((), ())), preferred_element_type=F32)


def _attn_prompt_kernel(qn_ref, qr_ref, gt_ref, kc_ref, vc_ref, slc_ref, win_ref, et_ref, mt_ref, o_ref,
                        *, n_sel, tk, cs):
    tq = qn_ref.shape[0]
    t_len = slc_ref.shape[1]
    kw = N_KV_HEADS * HEAD_DIM
    rpg = 4
    ns = 2 * rpg
    qb = pl.program_id(1)
    lo = lax.broadcasted_iota(jnp.int32, (tq, LANES), 1) < HEAD_DIM
    top = lax.broadcasted_iota(jnp.int32, (LANES, tq), 0) < HEAD_DIM
    wlen = min(WINDOW + tq, t_len)

    def q_of(shape):
        return qb * tq + lax.broadcasted_iota(jnp.int32, shape, 1)

    def parts_of(ref, p):
        parts = []
        for side in range(2):
            keep = lo if side == 0 else jnp.logical_not(lo)
            for r in range(rpg):
                c = p * rpg + r
                t = ref[:, c * LANES:(c + 1) * LANES]
                parts.append(jnp.where(keep, t, jnp.zeros_like(t)))
        return parts

    def slabs(x0, x1):
        return jnp.concatenate([x0] * rpg + [x1] * rpg, axis=1)

    def flash(kv_ref, q_chunks, n_full, tkk):
        n_ch = len(q_chunks)

        def tile(kt, carry, causal):
            ms, ls, accs = carry
            k0 = pl.multiple_of(kt * tkk, tkk)
            onehot = et_ref[pl.ds(k0, tkk), :]
            if causal:
                ok = (k0 + lax.broadcasted_iota(jnp.int32, (tkk, tq), 0)) <= q_of((tkk, tq))
                cb = jnp.concatenate([jnp.where(ok, 0.0, NEG)] * cs, axis=1)
            new_m, new_l, new_acc = [], [], []
            scores = []
            for p, qc in q_chunks:
                k = jnp.concatenate([kv_ref[0, pl.ds(k0, tkk), p * LANES:(p + 1) * LANES], onehot], axis=1)
                scores.append(_dot_nt(k, qc))
            for ci, (p, qc) in enumerate(q_chunks):
                v = kv_ref[0, pl.ds(k0, tkk), kw + p * LANES: kw + (p + 1) * LANES]
                s = scores[ci] + cb if causal else scores[ci]
                m_new = jnp.maximum(ms[ci], jnp.max(s, axis=0, keepdims=True))
                alpha = jnp.exp(ms[ci] - m_new)
                pe = jnp.exp(s - m_new)
                new_m.append(m_new)
                new_l.append(alpha * ls[ci] + jnp.sum(pe, axis=0, keepdims=True))
                new_acc.append(alpha * accs[ci] + _dot_tn(v, pe.astype(MXU_DTYPE)))
            return tuple(new_m), tuple(new_l), tuple(new_acc)

        init = (tuple(jnp.full((1, cs * tq), NEG, F32) for _ in range(n_ch)),
                tuple(jnp.zeros((1, cs * tq), F32) for _ in range(n_ch)),
                tuple(jnp.zeros((LANES, cs * tq), F32) for _ in range(n_ch)))
        carry = lax.fori_loop(0, n_full, functools.partial(tile, causal=False), init)
        _, ls, accs = tile(n_full, carry, True)
        outs = [a / jnp.maximum(l, 1e-30) for a, l in zip(accs, ls)]
        per_pair = n_ch // 2
        return [jnp.concatenate(outs[p * per_pair:(p + 1) * per_pair], axis=1) for p in range(2)]

    qr_parts = [parts_of(qr_ref, p) for p in range(2)]
    k0 = pl.multiple_of(jnp.maximum(qb * tq + tq - wlen, 0), tq)
    kpos = k0 + lax.broadcasted_iota(jnp.int32, (wlen, tq), 0)
    qp = q_of((wlen, tq))
    wb = jnp.where((kpos <= qp) & (kpos > qp - WINDOW), 0.0, NEG)
    o_win = []
    for p in range(2):
        k = win_ref[0, pl.ds(k0, wlen), p * LANES:(p + 1) * LANES]
        v = win_ref[0, pl.ds(k0, wlen), kw + p * LANES: kw + (p + 1) * LANES]
        s = _dot_nt(k, jnp.concatenate(qr_parts[p], axis=0)) + slabs(wb, wb)
        pe = jnp.exp(s - jnp.max(s, axis=0, keepdims=True))
        o_win.append(_dot_tn(v, pe.astype(MXU_DTYPE)) / jnp.maximum(jnp.sum(pe, axis=0, keepdims=True), 1e-30))

    o_cmp, sel_pen = [], []
    for p in range(2):
        qn_st = jnp.concatenate(parts_of(qn_ref, p), axis=0)
        s = _dot_nt(kc_ref[0, :, p * LANES:(p + 1) * LANES], qn_st)
        n_cb = s.shape[0]
        cm1 = (lax.broadcasted_iota(jnp.int32, (n_cb, tq), 0) * CMP_STRIDE + (CMP_LEN - 1)) <= q_of((n_cb, tq))
        cmask = slabs(cm1, cm1)
        s = jnp.where(cmask, s, -jnp.inf)
        mx = jnp.max(s, axis=0, keepdims=True)
        mx = jnp.where(mx == -jnp.inf, 0.0, mx)
        e = jnp.where(cmask, jnp.exp(s - mx), 0.0)
        pr = e / jnp.maximum(jnp.sum(e, axis=0, keepdims=True), 1e-30)
        o_cmp.append(_dot_tn(vc_ref[0, :, p * LANES:(p + 1) * LANES], pr.astype(MXU_DTYPE)))

        for side in range(2):
            imp = pr[:, side * rpg * tq:(side * rpg + 1) * tq]
            for r in range(1, rpg):
                imp = imp + pr[:, (side * rpg + r) * tq:(side * rpg + r + 1) * tq]
            imp_t = _dot_exact_lhs(mt_ref[...], imp)[:n_sel]
            blk = lax.broadcasted_iota(jnp.int32, (n_sel, tq), 0)
            sel = _select_blocks(imp_t, blk, q_of((n_sel, tq)), 0, n_sel)
            pen = jnp.concatenate([jnp.where(sel, 0.0, NEG).astype(F32),
                                   jnp.full((LANES - n_sel, tq), NEG, F32)], axis=0)
            sel_pen.append(pen.T.astype(MXU_DTYPE))

    q_chunks = [(p, jnp.concatenate([jnp.concatenate([qr_parts[p][j + jj], sel_pen[2 * p + (j + jj) // rpg]], axis=1)
                                     for jj in range(cs)], axis=0))
                for p in range(2) for j in range(0, ns, cs)]
    o_slc = flash(slc_ref, q_chunks, (qb * tq + tq - 1) // tk, tk)

    gt = gt_ref[...]
    for p in range(2):
        for r in range(rpg):
            c = p * rpg + r
            h_lo = (2 * p) * rpg + r
            h_hi = (2 * p + 1) * rpg + r

            def gate(br):
                return jnp.where(top, gt[h_lo * N_BRANCH + br: h_lo * N_BRANCH + br + 1, :],
                                 gt[h_hi * N_BRANCH + br: h_hi * N_BRANCH + br + 1, :])

            def pick(x):
                return jnp.where(top, x[:, r * tq:(r + 1) * tq], x[:, (rpg + r) * tq:(rpg + r + 1) * tq])

            o = gate(0) * pick(o_cmp[p]) + gate(1) * pick(o_slc[p]) + gate(2) * pick(o_win[p])
            o_ref[:, c * LANES:(c + 1) * LANES] = o.T.astype(o_ref.dtype)


def _attn_prompt(qn, qr, gates_t, kc, vc, slc16, win16, e_t, mt, *, n_sel, tq=128, tk=512, cs=8):
    b, t, w = slc16.shape
    nq = t // tq
    tk = min(tk, t)
    dq = qn.shape[1]
    n_cb = kc.shape[1]
    assert tq == LANES and t % tk == 0 and tk % tq == 0
    qspec = pl.BlockSpec((tq, dq), lambda i, j: (i * nq + j, 0))
    seq = lambda shape: pl.BlockSpec(shape, lambda i, j: (i, 0, 0))
    return pl.pallas_call(
        functools.partial(_attn_prompt_kernel, n_sel=n_sel, tk=tk, cs=cs),
        grid=(b, nq),
        in_specs=[qspec, qspec, pl.BlockSpec((LANES, tq), lambda i, j: (0, i * nq + j)),
                  seq((1, n_cb, kc.shape[2])), seq((1, n_cb, vc.shape[2])),
                  seq((1, t, w)), seq((1, t, w)), _const_spec(e_t.shape), _const_spec(mt.shape)],
        out_specs=qspec, out_shape=jax.ShapeDtypeStruct((b * t, dq), MXU_DTYPE),
        compiler_params=_cparams(2), name="attn_prompt",
    )(qn, qr, gates_t, kc, vc, slc16, win16, e_t, mt)


def _attn_sample_kernel(pt_ref, qn_ref, qr_ref, gt_ref, gk_ref, rsum_ref, m_ref, e_ref, abn_ref, sn_ref,
                        wst_ref, wn_ref, *rest, n_pages, n_sel, past_len, t_new, emit_window):
    spb = qn_ref.shape[0]
    if emit_window:
        o_ref, wout_ref, pad_s = rest[2 * spb * n_pages:]
    else:
        (o_ref, pad_s), wout_ref = rest[2 * spb * n_pages:], None
    for s in range(spb):
        _attn_sample_one(s, qn_ref, qr_ref, gt_ref, gk_ref, rsum_ref, m_ref, e_ref, abn_ref, sn_ref, wst_ref, wn_ref,
                         rest[s * n_pages:(s + 1) * n_pages],
                         rest[(spb + s) * n_pages:(spb + s + 1) * n_pages], o_ref, wout_ref, pad_s,
                         n_pages=n_pages, n_sel=n_sel, past_len=past_len, t_new=t_new)


def _attn_sample_one(sq, qn_ref, qr_ref, gt_ref, gk_ref, rsum_ref, m_ref, e_ref, abn_ref, sn_ref, wst_ref, wn_ref,
                     ab_pages, slc_pages, o_ref, wout_ref, pad_s, *, n_pages, n_sel, past_len, t_new):
    kw = N_KV_HEADS * HEAD_DIM
    page = slc_pages[0].shape[2]
    nrow = qn_ref.shape[1]
    rows_per_g = nrow // N_KV_HEADS
    qn = qn_ref[sq]
    qr = qr_ref[sq]
    t_of_row = lax.broadcasted_iota(jnp.int32, (nrow, 1), 0) % t_new
    qpos = past_len + t_of_row

    def softmax(s, mask):
        s = jnp.where(mask, s, -jnp.inf)
        mx = jnp.max(s, axis=-1, keepdims=True)
        mx = jnp.where(mx == -jnp.inf, 0.0, mx)
        e = jnp.where(mask, jnp.exp(s - mx), 0.0)
        return e / jnp.maximum(jnp.sum(e, axis=-1, keepdims=True), 1e-30)

    n_ab = m_ref.shape[0]
    pieces = [r[0] for r in ab_pages] + [abn_ref[sq]]
    n_have = sum(x.shape[0] for x in pieces)
    pieces.append(jnp.zeros((n_ab - n_have, pieces[0].shape[1]), F32))
    kc, vc = _assemble_kc_vc(jnp.concatenate(pieces, axis=0), gk_ref[...])
    blk_c = lax.broadcasted_iota(jnp.int32, (nrow, n_ab), 1)
    cmask = (blk_c * CMP_STRIDE + (CMP_LEN - 1)) <= qpos
    pc = softmax(_dot_nt(qn, kc.astype(MXU_DTYPE)), cmask)
    o_cmp = _dot(pc.astype(MXU_DTYPE), vc.astype(MXU_DTYPE))

    imp = _dot_exact_lhs(rsum_ref[...], pc)
    imp_s = _dot_exact_rhs(imp, m_ref[...])
    blk = lax.broadcasted_iota(jnp.int32, imp_s.shape, 1)
    sel = _select_blocks(imp_s, blk, jnp.broadcast_to(qpos, imp_s.shape), 1, n_sel)
    al = _dot(jnp.where(sel, 1.0, 0.0).astype(MXU_DTYPE), e_ref[...])
    n_keys = al.shape[1]
    kpos = lax.broadcasted_iota(jnp.int32, (nrow, n_keys), 1)
    smask = (al > 0.5) & (kpos <= qpos)

    pad_s[2 * sq] = jnp.zeros(pad_s.shape[1:], F32)
    pad_s[2 * sq, 0:sn_ref.shape[1], :] = sn_ref[sq]
    new_rows = pad_s[2 * sq]
    s = jnp.concatenate([_dot(qr, r[0, :kw, :].astype(MXU_DTYPE)) for r in slc_pages]
                        + [_dot_nt(qr, new_rows[:, :kw].astype(MXU_DTYPE))], axis=1)
    ps = softmax(s, smask).astype(MXU_DTYPE)
    o_slc = _dot(ps[:, n_pages * page:], new_rows[:, kw:].astype(MXU_DTYPE))
    for i, r in enumerate(slc_pages):
        o_slc = o_slc + _dot_nt(ps[:, i * page:(i + 1) * page], r[0, kw:, :].astype(MXU_DTYPE))

    wbuf = wst_ref.shape[2]
    pad_s[2 * sq + 1] = jnp.zeros(pad_s.shape[1:], F32)
    pad_s[2 * sq + 1, 0:wn_ref.shape[1], :] = wn_ref[sq]
    new_rows = pad_s[2 * sq + 1]
    s = jnp.concatenate([_dot(qr, wst_ref[sq, :kw, :].astype(MXU_DTYPE)),
                         _dot_nt(qr, new_rows[:, :kw].astype(MXU_DTYPE))], axis=1)
    wpos = (past_len - wbuf) + lax.broadcasted_iota(jnp.int32, s.shape, 1)
    wmask = (wpos <= qpos) & (wpos > qpos - WINDOW) & (wpos >= 0)
    pw = softmax(s, wmask).astype(MXU_DTYPE)
    o_win = (_dot_nt(pw[:, :wbuf], wst_ref[sq, kw:, :].astype(MXU_DTYPE))
             + _dot(pw[:, wbuf:], new_rows[:, kw:].astype(MXU_DTYPE)))

    gt = gt_ref[sq]
    o = gt[:, 0:1] * o_cmp + gt[:, 1:2] * o_slc + gt[:, 2:3] * o_win
    lane_g = lax.broadcasted_iota(jnp.int32, (rows_per_g, kw), 1) // HEAD_DIM
    acc = jnp.zeros((rows_per_g, kw), F32)
    for g in range(N_KV_HEADS):
        acc = acc + jnp.where(lane_g == g, o[g * rows_per_g:(g + 1) * rows_per_g], 0.0)
    o_ref[sq] = acc

    if wout_ref is not None:
        rolled = pltpu.roll(wst_ref[sq], wbuf - t_new, 1)
        wout_ref[sq, :, 0:wbuf - LANES] = rolled[:, 0:wbuf - LANES]
        new_t = pltpu.roll(new_rows.T, LANES - t_new, 1)
        ln = lax.broadcasted_iota(jnp.int32, new_t.shape, 1)
        wout_ref[sq, :, wbuf - LANES:wbuf] = jnp.where(ln >= LANES - t_new, new_t, rolled[:, wbuf - LANES:wbuf])


def _attn_sample(page_table, qn, qr, gates, gk, rsum, m_mat, e_mat, ab_pool, ab_new, cache_slc, slc_new,
                 win_state, win_new, *, n_sel, past_len, t_new, emit_window, seqs_per_step=2):
    nb, n_pages = page_table.shape
    nrow, kw = qn.shape[1], qn.shape[2]
    width = cache_slc.shape[1]
    page = cache_slc.shape[2]
    wbuf = win_state.shape[2]
    spb = seqs_per_step if nb % seqs_per_step == 0 else 1
    per_b = lambda shape: pl.BlockSpec((spb,) + shape, lambda i, pt: (i, 0, 0))

    def page_spec(shape, sq, p):
        return pl.BlockSpec(shape, lambda i, pt: (pt[(i * spb + sq) * n_pages + p], 0, 0))

    in_specs = [per_b((nrow, kw)), per_b((nrow, kw)), per_b((nrow, N_BRANCH)),
                pl.BlockSpec((1, LANES), lambda i, pt: (0, 0)),
                pl.BlockSpec(rsum.shape, lambda i, pt: (0, 0)),
                pl.BlockSpec(m_mat.shape, lambda i, pt: (0, 0)),
                pl.BlockSpec(e_mat.shape, lambda i, pt: (0, 0)),
                per_b(ab_new.shape[1:]), per_b(slc_new.shape[1:]),
                per_b((width, wbuf)), per_b(win_new.shape[1:])]
    in_specs += [page_spec((1,) + ab_pool.shape[1:], sq, p) for sq in range(spb) for p in range(n_pages)]
    in_specs += [page_spec((1, width, page), sq, p) for sq in range(spb) for p in range(n_pages)]
    out_specs = [per_b((nrow // N_KV_HEADS, kw))]
    out_shape = [jax.ShapeDtypeStruct((nb, nrow // N_KV_HEADS, kw), F32)]
    if emit_window:
        out_specs.append(per_b((width, wbuf)))
        out_shape.append(jax.ShapeDtypeStruct((nb, width, wbuf), F32))
    grid_spec = pltpu.PrefetchScalarGridSpec(
        num_scalar_prefetch=1, grid=(nb // spb,), in_specs=in_specs, out_specs=out_specs,
        scratch_shapes=[pltpu.VMEM((2 * spb, page, width), F32)])
    res = pl.pallas_call(
        functools.partial(_attn_sample_kernel, n_pages=n_pages, n_sel=n_sel, past_len=past_len, t_new=t_new,
                          emit_window=emit_window),
        grid_spec=grid_spec, out_shape=out_shape,
        compiler_params=_cparams(1), name="attn_sample",
    )(page_table.reshape(-1), qn, qr, gates, gk, rsum, m_mat, e_mat, ab_new, slc_new, win_state, win_new,
      *([ab_pool] * (spb * n_pages)), *([cache_slc] * (spb * n_pages)))
    return (res[0], res[1]) if emit_window else (res[0], None)


def _rope_tables(pos):
    half = ROPE_DIM // 2
    inv_freq = ROPE_THETA ** (-jnp.arange(half, dtype=F32) / half)
    ang = pos.astype(F32)[:, None] * inv_freq[None, :]
    cos, sin = jnp.cos(ang), jnp.sin(ang)
    n = pos.shape[0]
    z = jnp.zeros((n, HEAD_DIM - ROPE_DIM), F32)
    zh = jnp.zeros((n, half), F32)
    c_head = jnp.concatenate([cos, cos, z + 1.0], axis=1)
    dn_head = jnp.concatenate([zh, sin, z], axis=1)
    up_head = jnp.concatenate([-sin, zh, z], axis=1)
    tile2 = lambda a: jnp.concatenate([a, a], axis=1)
    return tile2(c_head), tile2(dn_head), tile2(up_head)


def _sel_matrices(n_cmp, n_cmp_pad, n_sel, n_keys):
    i = jnp.arange(n_cmp_pad)[:, None]
    j = jnp.arange(LANES)[None, :]
    m = ((i * CMP_STRIDE < j * SEL_BLOCK + SEL_BLOCK) & (i * CMP_STRIDE + CMP_LEN > j * SEL_BLOCK)
         & (i < n_cmp) & (j < n_sel))
    e = (jnp.arange(n_keys)[None, :] // SEL_BLOCK == jnp.arange(LANES)[:, None]) & (jnp.arange(LANES)[:, None] < n_sel)
    return m.astype(MXU_DTYPE), e.astype(MXU_DTYPE)


def _tile2(g):
    return jnp.concatenate([g, g]).reshape(1, LANES).astype(F32)


def kernel(x_prompt, x_sample, cache_kv_cmp, cache_kv_slc, state_kv_win, page_table, norm_mix, norm_ffn,
           ffn_w_gate, ffn_w_up, ffn_w_down, a_w_in, a_ln_g, a_ln_b, a_w_spatial, a_b_spatial, a_w_out,
           kv_norm, kv_w, k_norm_cmp, k_norm_slc, k_norm_win, cmp_pos_k, cmp_pos_v, cmp_w_k, cmp_w_v,
           b_w_in, b_q_norm, b_w_out):
    bf = MXU_DTYPE
    depth = norm_mix.shape[0]
    n_a = a_w_in.shape[0]
    bsz, t_p, d_model = x_prompt.shape
    nb, t_s, _ = x_sample.shape
    n_pages = page_table.shape[1]
    page = cache_kv_cmp.shape[1]
    past_len = n_pages * page
    width = 2 * N_KV_HEADS * HEAD_DIM
    kw = N_KV_HEADS * HEAD_DIM
    n_heads = b_w_out.shape[1] // HEAD_DIM
    rpg = n_heads // N_KV_HEADS
    dq = n_heads * HEAD_DIM
    assert N_KV_HEADS == 4 and rpg == 4 and t_p % CHUNK == 0 and CHUNK % t_s == 0 and t_s <= 8

    wg, wu, wd = ffn_w_gate.astype(bf), ffn_w_up.astype(bf), ffn_w_down.astype(bf)
    a_in, a_out = a_w_in.astype(bf), a_w_out.astype(bf)
    tril = jnp.tril(jnp.ones((CHUNK, CHUNK), bool))
    ws_p = jnp.where(tril[None, None], a_w_spatial, 0.0).astype(bf)
    bs_p = jnp.swapaxes(a_b_spatial, 1, 2)
    rep = CHUNK // t_s
    ws4 = jnp.where(tril[None, None, :t_s, :t_s], a_w_spatial[:, :, :t_s, :t_s], 0.0)
    ws_s = jnp.einsum('ab,lgts->lgatbs', jnp.eye(rep, dtype=F32), ws4).reshape(n_a, A_GROUPS, CHUNK, CHUNK).astype(bf)
    bs_s = jnp.tile(jnp.swapaxes(a_b_spatial[:, :, :t_s], 1, 2), (1, rep, 1))
    perm = jnp.array([(2 * p + side) * rpg + r for p in range(2) for r in range(rpg) for side in range(2)])
    wq = b_w_in[:, :, :dq].reshape(-1, d_model, n_heads, HEAD_DIM)[:, :, perm].reshape(-1, d_model, dq).astype(bf)
    wgate = jnp.pad(b_w_in[:, :, dq:], ((0, 0), (0, 0), (0, LANES - N_BRANCH * n_heads))).astype(bf)
    wo_pair = b_w_out.reshape(-1, n_heads, HEAD_DIM, d_model)[:, perm].reshape(-1, dq, d_model).astype(bf)
    wo_flat = b_w_out.astype(bf)
    kvw = kv_w.astype(bf)
    eye_g = jnp.eye(N_KV_HEADS, dtype=F32)
    w_bd = jnp.stack([jnp.einsum('ab,lde->ladbe', eye_g, w).reshape(CMP_LEN, kw, kw)
                      for w in (cmp_w_k, cmp_w_v)]).astype(bf)
    pos_tab = jnp.concatenate([jnp.tile(cmp_pos_k, (1, N_KV_HEADS)), jnp.tile(cmp_pos_v, (1, N_KV_HEADS))], axis=1)
    gk_cmp, gk_slc, gk_win = _tile2(k_norm_cmp), _tile2(k_norm_slc), _tile2(k_norm_win)

    def dense_a_layers(h, ws_eff, bs_eff, emit_v):
        vs = []
        for l in range(n_a):
            h, v = _gmlp_layer(h, norm_mix[l], a_in[l], a_ln_g[l], a_ln_b[l], ws_eff[l], bs_eff[l], a_out[l],
                               emit_v=emit_v)
            vs.append(v)
            h = _ffn_layer(h, norm_ffn[l], wg[l], wu[l], wd[l])
        return h, vs

    tabs_p = _rope_tables(jnp.arange(t_p))
    h = x_prompt.reshape(bsz * t_p, d_model)
    h, _ = dense_a_layers(h, ws_p, bs_p, False)
    kv_cmp_p, kv_slc_p, kv_win_p, cmp_slabs, slc16, win16 = _kv_project(h, kv_norm, kvw, gk_slc, gk_win, tabs_p,
                                                                        seq_len=t_p)
    ab = _compress_partial(cmp_slabs, pos_tab, w_bd)
    n_sub = t_p // CMP_STRIDE
    kc, vc = _prompt_kc_vc(ab.reshape(bsz, n_sub, 2 * width), gk_cmp)
    n_sel_p = t_p // SEL_BLOCK
    m_p, e_p = _sel_matrices(n_sub - 1, n_sub, n_sel_p, t_p)
    for j in range(depth - n_a):
        l = n_a + j
        qn, qr, gates = _q_project(h, norm_mix[l], wq[j], wgate[j], _tile2(b_q_norm[j]) * ATTN_SCALE, tabs_p)
        o = _attn_prompt(qn, qr, gates, kc, vc, slc16.reshape(bsz, t_p, width), win16.reshape(bsz, t_p, width),
                         e_p.T, m_p.T, n_sel=n_sel_p)
        h = _ffn_layer(h, norm_ffn[l], wg[l], wu[l], wd[l], attn=o, w_o=wo_pair[j])
    y_prompt = h.reshape(bsz, t_p, d_model)
    row5 = lambda a, b_, t_: a.reshape(b_, t_, 2, N_KV_HEADS, HEAD_DIM)
    from_cm = lambda a: jnp.transpose(a.reshape(a.shape[0], 2, N_KV_HEADS, HEAD_DIM, a.shape[2]), (0, 4, 1, 2, 3))
    to_cm = lambda a: jnp.transpose(a, (0, 2, 3, 4, 1)).reshape(a.shape[0], width, a.shape[1])
    w_keep = min(WINDOW, t_p)
    kv_win_prompt = from_cm(kv_win_p[:, :, t_p - w_keep:])

    m_s = nb * t_s
    tabs_s = _rope_tables(past_len + (jnp.arange(nb * t_s) % t_s))
    hs = x_sample.reshape(m_s, d_model)
    hs, vs = dense_a_layers(hs, ws_s, bs_s, True)
    a_v_sample = jnp.stack(vs).reshape(n_a, nb, t_s, -1)
    kv_cmp_s, kv_slc_s, kv_win_s, cmp_slabs_s = _kv_project(hs, kv_norm, kvw, gk_slc, gk_win, tabs_s)
    n_pool = cache_kv_cmp.shape[0]
    sub_pp = page // CMP_STRIDE
    ab_pool = _compress_pages(to_cm(cache_kv_cmp), pos_tab, w_bd).reshape(n_pool, sub_pp, 2 * width)
    t_full = -(-(past_len + t_s) // SEL_BLOCK) * SEL_BLOCK
    n_new_sub = (t_full - past_len) // CMP_STRIDE
    new_cmp = jnp.pad(cmp_slabs_s.reshape(-1, nb, t_s, LANES), ((0, 0), (0, 0), (0, t_full - past_len - t_s), (0, 0)))
    ab_new = _compress_partial(new_cmp.reshape(-1, nb * (t_full - past_len), LANES), pos_tab, w_bd)
    ab_new = jnp.pad(ab_new.reshape(nb, n_new_sub, 2 * width), ((0, 0), (0, 8 - n_new_sub), (0, 0)))
    n_cmp_s = t_full // CMP_STRIDE - 1
    n_sel_s = t_full // SEL_BLOCK
    n_cmp_pad = 256
    n_keys = past_len + page
    m_sm, e_sm = _sel_matrices(n_cmp_s, n_cmp_pad, n_sel_s, n_keys)
    nrow = n_heads * t_s
    ridx = jnp.arange(nrow)
    rsum = ((ridx[:, None] // (rpg * t_s) == ridx[None, :] // (rpg * t_s))
            & (ridx[:, None] % t_s == ridx[None, :] % t_s)).astype(bf)
    pad8 = lambda a: jnp.pad(a.reshape(nb, t_s, width), ((0, 0), (0, 8 - t_s), (0, 0)))
    slc_new, win_new = pad8(kv_slc_s), pad8(kv_win_s)
    inv_perm = jnp.argsort(perm)

    def pad_q(q):
        q = q.reshape(nb, t_s, n_heads, HEAD_DIM)[:, :, inv_perm].reshape(nb, t_s, N_KV_HEADS, rpg, HEAD_DIM)
        q = jnp.transpose(q, (0, 2, 3, 1, 4))
        q = jnp.einsum('bgrtd,gk->bgrtkd', q, jnp.eye(N_KV_HEADS, dtype=q.dtype))
        return q.reshape(nb, nrow, kw)

    slc_cm, win_cm = to_cm(cache_kv_slc), to_cm(state_kv_win)
    kv_win_sample = None
    for j in range(depth - n_a):
        l = n_a + j
        qn, qr, gates = _q_project(hs, norm_mix[l], wq[j], wgate[j], _tile2(b_q_norm[j]) * ATTN_SCALE, tabs_s)
        g3 = gates[:N_BRANCH * n_heads].T.reshape(nb, t_s, N_KV_HEADS, rpg, N_BRANCH)
        g3 = jnp.transpose(g3, (0, 2, 3, 1, 4)).reshape(nb, nrow, N_BRANCH)
        o, w_out = _attn_sample(page_table, pad_q(qn), pad_q(qr), g3, gk_cmp, rsum, m_sm, e_sm,
                                ab_pool, ab_new, slc_cm, slc_new, win_cm, win_new,
                                n_sel=n_sel_s, past_len=past_len, t_new=t_s, emit_window=j == 0)
        if j == 0:
            kv_win_sample = from_cm(w_out)
        o = jnp.transpose(o.reshape(nb, rpg, t_s, N_KV_HEADS, HEAD_DIM), (0, 2, 3, 1, 4)).reshape(m_s, dq)
        hs = _ffn_layer(hs, norm_ffn[l], wg[l], wu[l], wd[l], attn=o.astype(bf), w_o=wo_flat[j])
    y_sample = hs.reshape(nb, t_s, d_model)

    return (y_prompt, y_sample, from_cm(kv_cmp_p), row5(kv_cmp_s, nb, t_s),
            from_cm(kv_slc_p), row5(kv_slc_s, nb, t_s), kv_win_prompt, kv_win_sample, a_v_sample)
```

```python
import functools

import jax
import jax.numpy as jnp
from jax import lax
from jax.experimental import pallas as pl
from jax.experimental.pallas import tpu as pltpu

CHUNK = 128
A_GROUPS = 8
HEAD_DIM = 64
N_KV_HEADS = 4
ROPE_DIM = HEAD_DIM // 4
ROPE_THETA = 500000.0
CMP_STRIDE = 16
CMP_LEN = 2 * CMP_STRIDE
SEL_BLOCK = 64
SEL_TOPK = 16
WINDOW = 512
N_BRANCH = 3
EPS = 1e-6
ATTN_SCALE = HEAD_DIM ** -0.5

LANES = 128
PAGE_SUB_PITCH = 24
VMEM_LIMIT_BYTES = 56 * 1024 * 1024
MXU_DTYPE = jnp.bfloat16
NEG = -1e30
F32 = jnp.float32


def _cparams(n_grid):
    return pltpu.CompilerParams(dimension_semantics=("arbitrary",) * n_grid,
                                vmem_limit_bytes=VMEM_LIMIT_BYTES)


def _const_spec(shape):
    nd = len(shape)
    return pl.BlockSpec(shape, lambda *_: (0,) * nd, pipeline_mode=pl.Buffered(1))


def _dot(a, b):
    return jnp.dot(a, b, preferred_element_type=F32)


def _dot_nt(a, b):
    return lax.dot_general(a, b, (((1,), (1,)), ((), ())), preferred_element_type=F32)


def _split3(x):
    hi = x.astype(MXU_DTYPE)
    r1 = x - hi.astype(F32)
    mid = r1.astype(MXU_DTYPE)
    lo = (r1 - mid.astype(F32)).astype(MXU_DTYPE)
    return hi, mid, lo


def _dot_exact_lhs(a, x):
    hi, mid, lo = _split3(x)
    return _dot(a, hi) + _dot(a, mid) + _dot(a, lo)


def _dot_exact_rhs(x, b):
    hi, mid, lo = _split3(x)
    return _dot(hi, b) + _dot(mid, b) + _dot(lo, b)


def _rms(x, g):
    return x * lax.rsqrt(jnp.mean(x * x, axis=-1, keepdims=True) + EPS) * g


def _headnorm128(x, g):
    lo = lax.broadcasted_iota(jnp.int32, x.shape, 1) < HEAD_DIM
    sq = x * x
    s_lo = jnp.sum(jnp.where(lo, sq, 0.0), axis=-1, keepdims=True)
    s_hi = jnp.sum(jnp.where(lo, 0.0, sq), axis=-1, keepdims=True)
    ms = jnp.where(lo, s_lo, s_hi) * (1.0 / HEAD_DIM)
    return x * lax.rsqrt(ms + EPS) * g


def _rope128(x, cos, s_dn, s_up):
    half = ROPE_DIM // 2
    return x * cos + pltpu.roll(x, half, 1) * s_dn + pltpu.roll(x, LANES - half, 1) * s_up


def _gmlp_kernel(h_ref, g_ref, win_ref, lng_ref, lnb_ref, ws_ref, bs_ref, wout_ref, *rest, emit_v):
    if emit_v:
        o_ref, v_ref, gated = rest
    else:
        o_ref, gated = rest
    tm = h_ref.shape[0]
    d_a = lng_ref.shape[1]
    dg = d_a // A_GROUPS
    x = h_ref[...]
    xn = _rms(x, g_ref[...]).astype(MXU_DTYPE)
    v = _dot(xn, win_ref[:, d_a:])
    u = _dot(xn, win_ref[:, :d_a])
    v = jax.nn.gelu(v)
    vc = v - jnp.mean(v, axis=-1, keepdims=True)
    v = vc * lax.rsqrt(jnp.mean(vc * vc, axis=-1, keepdims=True) + EPS) * lng_ref[...] + lnb_ref[...]
    if emit_v:
        v_ref[...] = v
    vb = v.astype(MXU_DTYPE)
    for c in range(tm // CHUNK):
        r0, r1 = c * CHUNK, (c + 1) * CHUNK
        for g in range(A_GROUPS):
            c0, c1 = g * dg, (g + 1) * dg
            sv = _dot(ws_ref[g], vb[r0:r1, c0:c1]) + bs_ref[:, g:g + 1]
            gated[r0:r1, c0:c1] = (jax.nn.gelu(u[r0:r1, c0:c1]) * sv).astype(MXU_DTYPE)
    o_ref[...] = x + _dot(gated[...], wout_ref[...])


def _gmlp_layer(h, g, w_in, ln_g, ln_b, ws_eff, bs_eff, w_out, *, emit_v, tm=512):
    m, d = h.shape
    tm = min(tm, m)
    d_a = ln_g.shape[-1]
    row = pl.BlockSpec((tm, d), lambda i: (i, 0))
    out_shape = [jax.ShapeDtypeStruct((m, d), F32)]
    out_specs = [row]
    if emit_v:
        out_shape.append(jax.ShapeDtypeStruct((m, d_a), F32))
        out_specs.append(pl.BlockSpec((tm, d_a), lambda i: (i, 0)))
    res = pl.pallas_call(
        functools.partial(_gmlp_kernel, emit_v=emit_v),
        grid=(m // tm,),
        in_specs=[row, _const_spec((1, d)), _const_spec(w_in.shape), _const_spec((1, d_a)),
                  _const_spec((1, d_a)), _const_spec(ws_eff.shape), _const_spec(bs_eff.shape),
                  _const_spec(w_out.shape)],
        out_specs=out_specs, out_shape=out_shape,
        scratch_shapes=[pltpu.VMEM((tm, d_a), MXU_DTYPE)],
        compiler_params=_cparams(1), name="gmlp_layer",
    )(h, g.reshape(1, d), w_in, ln_g.reshape(1, d_a), ln_b.reshape(1, d_a), ws_eff, bs_eff, w_out)
    return (res[0], res[1]) if emit_v else (res[0], None)


def _ffn_kernel(*refs, with_proj):
    if with_proj:
        h_ref, a_ref, wo_ref, g_ref, wg_ref, wu_ref, wd_ref, o_ref = refs
        x = h_ref[...] + _dot(a_ref[...], wo_ref[...])
    else:
        h_ref, g_ref, wg_ref, wu_ref, wd_ref, o_ref = refs
        x = h_ref[...]
    xn = _rms(x, g_ref[...]).astype(MXU_DTYPE)
    a, b = _dot(xn, wg_ref[...]), _dot(xn, wu_ref[...])
    mid = (jax.nn.silu(a) * b).astype(MXU_DTYPE)
    o_ref[...] = x + _dot(mid, wd_ref[...])


def _ffn_layer(h, g, wg, wu, wd, attn=None, w_o=None, *, tm=512):
    m, d = h.shape
    tm = min(tm, m)
    row = pl.BlockSpec((tm, d), lambda i: (i, 0))
    with_proj = attn is not None
    args, specs = [h], [row]
    if with_proj:
        args += [attn, w_o]
        specs += [pl.BlockSpec((tm, attn.shape[1]), lambda i: (i, 0)), _const_spec(w_o.shape)]
    args += [g.reshape(1, d), wg, wu, wd]
    specs += [_const_spec((1, d)), _const_spec(wg.shape), _const_spec(wu.shape), _const_spec(wd.shape)]
    return pl.pallas_call(
        functools.partial(_ffn_kernel, with_proj=with_proj),
        grid=(m // tm,), in_specs=specs, out_specs=row,
        out_shape=jax.ShapeDtypeStruct((m, d), F32),
        compiler_params=_cparams(1), name="ffn_layer",
    )(*args)


def _kvproj_kernel(h_ref, g_ref, w_ref, gs_ref, gw_ref, cos_ref, sdn_ref, sup_ref,
                   cmp_ref, slc_ref, win_ref, slabs_ref, *bf_refs, channel_major):
    x = h_ref[...]
    xn = _rms(x, g_ref[...]).astype(MXU_DTYPE)
    kv = _dot(xn, w_ref[...])
    width = 2 * N_KV_HEADS * HEAD_DIM
    kw = N_KV_HEADS * HEAD_DIM
    cos, sdn, sup = cos_ref[...], sdn_ref[...], sup_ref[...]

    def emit(o_ref, c, tile):
        if channel_major:
            o_ref[0, c * LANES:(c + 1) * LANES, :] = tile.T
        else:
            o_ref[:, c * LANES:(c + 1) * LANES] = tile

    for c in range(width // LANES):
        tile = kv[:, c * LANES:(c + 1) * LANES]
        emit(cmp_ref, c, tile)
        slabs_ref[c] = tile
    for bi, (base, gk_ref, o_ref) in enumerate(((width, gs_ref, slc_ref), (2 * width, gw_ref, win_ref))):
        for c in range(width // LANES):
            tile = kv[:, base + c * LANES: base + (c + 1) * LANES]
            if c < kw // LANES:
                tile = _rope128(_headnorm128(tile, gk_ref[...]), cos, sdn, sup)
            emit(o_ref, c, tile)
            if channel_major:
                bf_refs[bi][:, c * LANES:(c + 1) * LANES] = tile.astype(MXU_DTYPE)


def _kv_project(h, kv_norm, kv_w, gk_slc, gk_win, tabs, *, seq_len=None, tm=512):
    m, d = h.shape
    tm = min(tm, m)
    width = 2 * N_KV_HEADS * HEAD_DIM
    n_tab = tabs[0].shape[0] // tm
    row = pl.BlockSpec((tm, d), lambda i: (i, 0))
    tab = pl.BlockSpec((tm, LANES), lambda i: (i % n_tab, 0))
    orow = pl.BlockSpec((tm, width), lambda i: (i, 0))
    slab = pl.BlockSpec((width // LANES, tm, LANES), lambda i: (0, i, 0))
    slab_shape = jax.ShapeDtypeStruct((width // LANES, m, LANES), F32)
    if seq_len is not None:
        nblk = seq_len // tm
        ocm = pl.BlockSpec((1, width, tm), lambda i: (i // nblk, 0, i % nblk))
        out_specs = [ocm] * 3 + [slab, orow, orow]
        out_shape = ([jax.ShapeDtypeStruct((m // seq_len, width, seq_len), F32)] * 3 + [slab_shape]
                     + [jax.ShapeDtypeStruct((m, width), MXU_DTYPE)] * 2)
    else:
        out_specs = [orow] * 3 + [slab]
        out_shape = [jax.ShapeDtypeStruct((m, width), F32)] * 3 + [slab_shape]
    return pl.pallas_call(
        functools.partial(_kvproj_kernel, channel_major=seq_len is not None), grid=(m // tm,),
        in_specs=[row, _const_spec((1, d)), _const_spec(kv_w.shape), _const_spec((1, LANES)),
                  _const_spec((1, LANES)), tab, tab, tab],
        out_specs=out_specs, out_shape=out_shape,
        compiler_params=_cparams(1), name="kv_project",
    )(h, kv_norm.reshape(1, d), kv_w, gk_slc, gk_win, *tabs)


def _qproj_kernel(h_ref, g_ref, wq_ref, wgt_ref, gq_ref, cos_ref, sdn_ref, sup_ref,
                  qn_ref, qr_ref, gates_ref):
    x = h_ref[...]
    xn = _rms(x, g_ref[...]).astype(MXU_DTYPE)
    q = _dot(xn, wq_ref[...])
    gates_ref[...] = jax.nn.sigmoid(_dot(xn, wgt_ref[...])).T
    cos, sdn, sup = cos_ref[...], sdn_ref[...], sup_ref[...]
    for c in range(q.shape[1] // LANES):
        qn = _headnorm128(q[:, c * LANES:(c + 1) * LANES], gq_ref[...])
        qn_ref[:, c * LANES:(c + 1) * LANES] = qn.astype(qn_ref.dtype)
        qr_ref[:, c * LANES:(c + 1) * LANES] = _rope128(qn, cos, sdn, sup).astype(qr_ref.dtype)


def _q_project(h, g, w_q, w_gate, gq, tabs, *, tm=512):
    m, d = h.shape
    tm = min(tm, m)
    dq = w_q.shape[1]
    n_tab = tabs[0].shape[0] // tm
    row = pl.BlockSpec((tm, d), lambda i: (i, 0))
    tab = pl.BlockSpec((tm, LANES), lambda i: (i % n_tab, 0))
    qrow = pl.BlockSpec((tm, dq), lambda i: (i, 0))
    return pl.pallas_call(
        _qproj_kernel, grid=(m // tm,),
        in_specs=[row, _const_spec((1, d)), _const_spec(w_q.shape), _const_spec(w_gate.shape),
                  _const_spec((1, LANES)), tab, tab, tab],
        out_specs=[qrow, qrow, pl.BlockSpec((LANES, tm), lambda i: (0, i))],
        out_shape=[jax.ShapeDtypeStruct((m, dq), MXU_DTYPE)] * 2 + [jax.ShapeDtypeStruct((LANES, m), F32)],
        compiler_params=_cparams(1), name="q_project",
    )(h, g.reshape(1, d), w_q, w_gate, gq, *tabs)


def _compress_from_slabs(x_ref, pos_ref, w_ref, o_ref, pitch=CMP_STRIDE, row0=0, mb=None, between=None):
    width = 2 * N_KV_HEADS * HEAD_DIM
    kw = N_KV_HEADS * HEAD_DIM
    mb = o_ref.shape[0] if mb is None else mb
    spk = kw // LANES
    for kv in range(2):
        acc_a = jnp.zeros((mb, kw), F32)
        acc_b = jnp.zeros((mb, kw), F32)
        for l in range(CMP_STRIDE):
            z = jnp.concatenate([x_ref[kv * spk + j, pl.ds(l, mb, stride=pitch), :] for j in range(spk)], axis=1)
            pa = pos_ref[l:l + 1, kv * kw:(kv + 1) * kw]
            pb = pos_ref[CMP_STRIDE + l:CMP_STRIDE + l + 1, kv * kw:(kv + 1) * kw]
            acc_a = acc_a + _dot((z + pa).astype(MXU_DTYPE), w_ref[kv, l])
            acc_b = acc_b + _dot((z + pb).astype(MXU_DTYPE), w_ref[kv, CMP_STRIDE + l])
            if between is not None:
                between(kv * CMP_STRIDE + l)
        o_ref[row0:row0 + mb, kv * kw:(kv + 1) * kw] = acc_a
        o_ref[row0:row0 + mb, width + kv * kw: width + (kv + 1) * kw] = acc_b


def _compress_pages_kernel(x_ref, pos_ref, w_ref, o_ref, xt_a, xt_b):
    n_pg, width, page = x_ref.shape
    sub_pp = page // CMP_STRIDE
    half = n_pg // 2
    mbh = half * sub_pp

    def transpose_page(xt, pg_src, pg_dst):
        r0 = pg_dst * (sub_pp * PAGE_SUB_PITCH)
        if not isinstance(r0, int):
            r0 = pl.multiple_of(r0, 8)
        for c in range(width // LANES):
            rows = x_ref[pg_src, c * LANES:(c + 1) * LANES, :].T
            for m in range(sub_pp):
                xt[c, pl.ds(r0 + m * PAGE_SUB_PITCH, CMP_STRIDE), :] = rows[m * CMP_STRIDE:(m + 1) * CMP_STRIDE]

    def body(pg, carry):
        transpose_page(xt_a, pg, pg)
        return carry

    lax.fori_loop(0, half, body, 0, unroll=4)
    steps = 2 * CMP_STRIDE
    per_step = -(-half // steps)

    def between(i):
        for j in range(i * per_step, min((i + 1) * per_step, half)):
            transpose_page(xt_b, half + j, j)

    _compress_from_slabs(xt_a, pos_ref, w_ref, o_ref, pitch=PAGE_SUB_PITCH, row0=0, mb=mbh, between=between)
    _compress_from_slabs(xt_b, pos_ref, w_ref, o_ref, pitch=PAGE_SUB_PITCH, row0=mbh, mb=mbh)


def _compress_partial(slabs, pos_tab, w_bd, *, mb=256):
    n_slab, n_rows, _ = slabs.shape
    n_sub = n_rows // CMP_STRIDE
    mb = min(mb, n_sub)
    width = 2 * N_KV_HEADS * HEAD_DIM
    return pl.pallas_call(
        _compress_from_slabs, grid=(n_sub // mb,),
        in_specs=[pl.BlockSpec((n_slab, mb * CMP_STRIDE, LANES), lambda i: (0, i, 0)),
                  _const_spec(pos_tab.shape), _const_spec(w_bd.shape)],
        out_specs=pl.BlockSpec((mb, 2 * width), lambda i: (i, 0)),
        out_shape=jax.ShapeDtypeStruct((n_sub, 2 * width), F32),
        compiler_params=_cparams(1), name="compress_partial",
    )(slabs, pos_tab, w_bd)


def _compress_pages(pages_cm, pos_tab, w_bd, *, pages_per_step=32):
    n_pool, width, page = pages_cm.shape
    pps = min(pages_per_step, n_pool)
    assert n_pool % pps == 0 and pps % 2 == 0
    mb = pps * page // CMP_STRIDE
    return pl.pallas_call(
        _compress_pages_kernel, grid=(n_pool // pps,),
        in_specs=[pl.BlockSpec((pps, width, page), lambda i: (i, 0, 0)),
                  _const_spec(pos_tab.shape), _const_spec(w_bd.shape)],
        out_specs=pl.BlockSpec((mb, 2 * width), lambda i: (i, 0)),
        out_shape=jax.ShapeDtypeStruct((n_pool * page // CMP_STRIDE, 2 * width), F32),
        scratch_shapes=[pltpu.VMEM((width // LANES, mb // 2 * PAGE_SUB_PITCH, LANES), F32)] * 2,
        compiler_params=_cparams(1), name="compress_pages",
    )(pages_cm, pos_tab, w_bd)


def _assemble_kc_vc(ab, gk):
    width = 2 * N_KV_HEADS * HEAD_DIM
    kw = N_KV_HEADS * HEAD_DIM
    n = ab.shape[0]
    kcv = ab[:, :width] + pltpu.roll(ab[:, width:], n - 1, 0)
    kc = jnp.concatenate([_headnorm128(kcv[:, c * LANES:(c + 1) * LANES], gk) for c in range(kw // LANES)],
                         axis=1)
    return kc, kcv[:, kw:width]


def _kcvc_kernel(ab_ref, gk_ref, kc_ref, vc_ref):
    kc, vc = _assemble_kc_vc(ab_ref[0], gk_ref[...])
    kc_ref[0] = kc.astype(kc_ref.dtype)
    vc_ref[0] = vc.astype(vc_ref.dtype)


def _prompt_kc_vc(ab, gk):
    b, n, w = ab.shape
    kw = N_KV_HEADS * HEAD_DIM
    spec = pl.BlockSpec((1, n, kw), lambda i: (i, 0, 0))
    return pl.pallas_call(
        _kcvc_kernel, grid=(b,),
        in_specs=[pl.BlockSpec((1, n, w), lambda i: (i, 0, 0)), _const_spec((1, LANES))],
        out_specs=[spec, spec], out_shape=[jax.ShapeDtypeStruct((b, n, kw), MXU_DTYPE)] * 2,
        compiler_params=_cparams(1), name="prompt_kc_vc",
    )(ab, gk)


def _select_blocks(imp, blk, qpos, axis, n_sel):
    cur = qpos // SEL_BLOCK
    valid = blk <= cur
    forced = (blk == 0) | (blk == cur) | (blk == cur - 1)
    sc = jnp.where(forced, jnp.inf, jnp.where(valid, imp, -jnp.inf))
    cnt = jnp.zeros(sc.shape, jnp.int32)
    for j in range(n_sel):
        sj = lax.slice_in_dim(sc, j, j + 1, axis=axis)
        beats = (sj > sc) | ((sj == sc) & (blk > j))
        cnt = cnt + beats.astype(jnp.int32)
    return (cnt < min(SEL_TOPK, n_sel)) & (sc > -jnp.inf) & (blk < n_sel)


def _dot_tn(a, b):
    return lax.dot_general(a, b, (((0,), (0,)), ((), ())), preferred_element_type=F32)


def _attn_prompt_kernel(qn_ref, qr_ref, gt_ref, kc_ref, vc_ref, slc_ref, win_ref, et_ref, mt_ref, o_ref,
                        *, n_sel, tk, cs):
    tq = qn_ref.shape[0]
    t_len = slc_ref.shape[1]
    kw = N_KV_HEADS * HEAD_DIM
    rpg = 4
    ns = 2 * rpg
    qb = pl.program_id(1)
    lo = lax.broadcasted_iota(jnp.int32, (tq, LANES), 1) < HEAD_DIM
    top = lax.broadcasted_iota(jnp.int32, (LANES, tq), 0) < HEAD_DIM
    wlen = min(WINDOW + tq, t_len)

    def q_of(shape):
        return qb * tq + lax.broadcasted_iota(jnp.int32, shape, 1)

    def parts_of(ref, p):
        parts = []
        for side in range(2):
            keep = lo if side == 0 else jnp.logical_not(lo)
            for r in range(rpg):
                c = p * rpg + r
                t = ref[:, c * LANES:(c + 1) * LANES]
                parts.append(jnp.where(keep, t, jnp.zeros_like(t)))
        return parts

    def slabs(x0, x1):
        return jnp.concatenate([x0] * rpg + [x1] * rpg, axis=1)

    def flash(kv_ref, q_chunks, n_full, tkk):
        n_ch = len(q_chunks)

        def tile(kt, carry, causal):
            ms, ls, accs = carry
            k0 = pl.multiple_of(kt * tkk, tkk)
            onehot = et_ref[pl.ds(k0, tkk), :]
            if causal:
                ok = (k0 + lax.broadcasted_iota(jnp.int32, (tkk, tq), 0)) <= q_of((tkk, tq))
                cb = jnp.concatenate([jnp.where(ok, 0.0, NEG)] * cs, axis=1)
            new_m, new_l, new_acc = [], [], []
            scores = []
            for p, qc in q_chunks:
                k = jnp.concatenate([kv_ref[0, pl.ds(k0, tkk), p * LANES:(p + 1) * LANES], onehot], axis=1)
                scores.append(_dot_nt(k, qc))
            for ci, (p, qc) in enumerate(q_chunks):
                v = kv_ref[0, pl.ds(k0, tkk), kw + p * LANES: kw + (p + 1) * LANES]
                s = scores[ci] + cb if causal else scores[ci]
                m_new = jnp.maximum(ms[ci], jnp.max(s, axis=0, keepdims=True))
                alpha = jnp.exp(ms[ci] - m_new)
                pe = jnp.exp(s - m_new)
                new_m.append(m_new)
                new_l.append(alpha * ls[ci] + jnp.sum(pe, axis=0, keepdims=True))
                new_acc.append(alpha * accs[ci] + _dot_tn(v, pe.astype(MXU_DTYPE)))
            return tuple(new_m), tuple(new_l), tuple(new_acc)

        init = (tuple(jnp.full((1, cs * tq), NEG, F32) for _ in range(n_ch)),
                tuple(jnp.zeros((1, cs * tq), F32) for _ in range(n_ch)),
                tuple(jnp.zeros((LANES, cs * tq), F32) for _ in range(n_ch)))
        carry = lax.fori_loop(0, n_full, functools.partial(tile, causal=False), init)
        _, ls, accs = tile(n_full, carry, True)
        outs = [a / jnp.maximum(l, 1e-30) for a, l in zip(accs, ls)]
        per_pair = n_ch // 2
        return [jnp.concatenate(outs[p * per_pair:(p + 1) * per_pair], axis=1) for p in range(2)]

    qr_parts = [parts_of(qr_ref, p) for p in range(2)]
    k0 = pl.multiple_of(jnp.maximum(qb * tq + tq - wlen, 0), tq)
    kpos = k0 + lax.broadcasted_iota(jnp.int32, (wlen, tq), 0)
    qp = q_of((wlen, tq))
    wb = jnp.where((kpos <= qp) & (kpos > qp - WINDOW), 0.0, NEG)
    o_win = []
    w_scores = [_dot_nt(win_ref[0, pl.ds(k0, wlen), p * LANES:(p + 1) * LANES], jnp.concatenate(qr_parts[p], axis=0))
                for p in range(2)]
    for p in range(2):
        v = win_ref[0, pl.ds(k0, wlen), kw + p * LANES: kw + (p + 1) * LANES]
        s = w_scores[p] + slabs(wb, wb)
        pe = jnp.exp(s - jnp.max(s, axis=0, keepdims=True))
        o_win.append(_dot_tn(v, pe.astype(MXU_DTYPE)) / jnp.maximum(jnp.sum(pe, axis=0, keepdims=True), 1e-30))

    o_cmp, sel_pen = [], []
    for p in range(2):
        qn_st = jnp.concatenate(parts_of(qn_ref, p), axis=0)
        s = _dot_nt(kc_ref[0, :, p * LANES:(p + 1) * LANES], qn_st)
        n_cb = s.shape[0]
        cm1 = (lax.broadcasted_iota(jnp.int32, (n_cb, tq), 0) * CMP_STRIDE + (CMP_LEN - 1)) <= q_of((n_cb, tq))
        cmask = slabs(cm1, cm1)
        s = jnp.where(cmask, s, -jnp.inf)
        mx = jnp.max(s, axis=0, keepdims=True)
        mx = jnp.where(mx == -jnp.inf, 0.0, mx)
        e = jnp.where(cmask, jnp.exp(s - mx), 0.0)
        pr = e / jnp.maximum(jnp.sum(e, axis=0, keepdims=True), 1e-30)
        o_cmp.append(_dot_tn(vc_ref[0, :, p * LANES:(p + 1) * LANES], pr.astype(MXU_DTYPE)))

        for side in range(2):
            imp = pr[:, side * rpg * tq:(side * rpg + 1) * tq]
            for r in range(1, rpg):
                imp = imp + pr[:, (side * rpg + r) * tq:(side * rpg + r + 1) * tq]
            imp_t = _dot_exact_lhs(mt_ref[...], imp)[:n_sel]
            blk = lax.broadcasted_iota(jnp.int32, (n_sel, tq), 0)
            sel = _select_blocks(imp_t, blk, q_of((n_sel, tq)), 0, n_sel)
            pen = jnp.concatenate([jnp.where(sel, 0.0, NEG).astype(F32),
                                   jnp.full((LANES - n_sel, tq), NEG, F32)], axis=0)
            sel_pen.append(pen.T.astype(MXU_DTYPE))

    q_chunks = [(p, jnp.concatenate([jnp.concatenate([qr_parts[p][j + jj], sel_pen[2 * p + (j + jj) // rpg]], axis=1)
                                     for jj in range(cs)], axis=0))
                for p in range(2) for j in range(0, ns, cs)]
    o_slc = flash(slc_ref, q_chunks, (qb * tq + tq - 1) // tk, tk)

    gt = gt_ref[...]
    for p in range(2):
        for r in range(rpg):
            c = p * rpg + r
            h_lo = (2 * p) * rpg + r
            h_hi = (2 * p + 1) * rpg + r

            def gate(br):
                return jnp.where(top, gt[h_lo * N_BRANCH + br: h_lo * N_BRANCH + br + 1, :],
                                 gt[h_hi * N_BRANCH + br: h_hi * N_BRANCH + br + 1, :])

            def pick(x):
                return jnp.where(top, x[:, r * tq:(r + 1) * tq], x[:, (rpg + r) * tq:(rpg + r + 1) * tq])

            o = gate(0) * pick(o_cmp[p]) + gate(1) * pick(o_slc[p]) + gate(2) * pick(o_win[p])
            o_ref[:, c * LANES:(c + 1) * LANES] = o.T.astype(o_ref.dtype)


def _attn_prompt(qn, qr, gates_t, kc, vc, slc16, win16, e_t, mt, *, n_sel, tq=128, tk=512, cs=8):
    b, t, w = slc16.shape
    nq = t // tq
    tk = min(tk, t)
    dq = qn.shape[1]
    n_cb = kc.shape[1]
    assert tq == LANES and t % tk == 0 and tk % tq == 0
    qspec = pl.BlockSpec((tq, dq), lambda i, j: (i * nq + j, 0))
    seq = lambda shape: pl.BlockSpec(shape, lambda i, j: (i, 0, 0))
    return pl.pallas_call(
        functools.partial(_attn_prompt_kernel, n_sel=n_sel, tk=tk, cs=cs),
        grid=(b, nq),
        in_specs=[qspec, qspec, pl.BlockSpec((LANES, tq), lambda i, j: (0, i * nq + j)),
                  seq((1, n_cb, kc.shape[2])), seq((1, n_cb, vc.shape[2])),
                  seq((1, t, w)), seq((1, t, w)), _const_spec(e_t.shape), _const_spec(mt.shape)],
        out_specs=qspec, out_shape=jax.ShapeDtypeStruct((b * t, dq), MXU_DTYPE),
        compiler_params=_cparams(2), name="attn_prompt",
    )(qn, qr, gates_t, kc, vc, slc16, win16, e_t, mt)


def _attn_sample_kernel(pt_ref, qn_ref, qr_ref, gt_ref, gk_ref, rsum_ref, m_ref, e_ref, abn_ref, sn_ref,
                        wst_ref, wn_ref, *rest, n_pages, n_sel, past_len, t_new, emit_window):
    spb = qn_ref.shape[0]
    if emit_window:
        o_ref, wout_ref, pad_s = rest[2 * spb * n_pages:]
    else:
        (o_ref, pad_s), wout_ref = rest[2 * spb * n_pages:], None
    for s in range(spb):
        _attn_sample_one(s, qn_ref, qr_ref, gt_ref, gk_ref, rsum_ref, m_ref, e_ref, abn_ref, sn_ref, wst_ref, wn_ref,
                         rest[s * n_pages:(s + 1) * n_pages],
                         rest[(spb + s) * n_pages:(spb + s + 1) * n_pages], o_ref, wout_ref, pad_s,
                         n_pages=n_pages, n_sel=n_sel, past_len=past_len, t_new=t_new)


def _attn_sample_one(sq, qn_ref, qr_ref, gt_ref, gk_ref, rsum_ref, m_ref, e_ref, abn_ref, sn_ref, wst_ref, wn_ref,
                     ab_pages, slc_pages, o_ref, wout_ref, pad_s, *, n_pages, n_sel, past_len, t_new):
    kw = N_KV_HEADS * HEAD_DIM
    page = slc_pages[0].shape[2]
    nrow = qn_ref.shape[1]
    rows_per_g = nrow // N_KV_HEADS
    qn = qn_ref[sq]
    qr = qr_ref[sq]
    t_of_row = lax.broadcasted_iota(jnp.int32, (nrow, 1), 0) % t_new
    qpos = past_len + t_of_row

    def softmax(s, mask):
        s = jnp.where(mask, s, -jnp.inf)
        mx = jnp.max(s, axis=-1, keepdims=True)
        mx = jnp.where(mx == -jnp.inf, 0.0, mx)
        e = jnp.where(mask, jnp.exp(s - mx), 0.0)
        return e / jnp.maximum(jnp.sum(e, axis=-1, keepdims=True), 1e-30)

    n_ab = m_ref.shape[0]
    pieces = [r[0] for r in ab_pages] + [abn_ref[sq]]
    n_have = sum(x.shape[0] for x in pieces)
    pieces.append(jnp.zeros((n_ab - n_have, pieces[0].shape[1]), F32))
    kc, vc = _assemble_kc_vc(jnp.concatenate(pieces, axis=0), gk_ref[...])
    blk_c = lax.broadcasted_iota(jnp.int32, (nrow, n_ab), 1)
    cmask = (blk_c * CMP_STRIDE + (CMP_LEN - 1)) <= qpos
    pc = softmax(_dot_nt(qn, kc.astype(MXU_DTYPE)), cmask)
    o_cmp = _dot(pc.astype(MXU_DTYPE), vc.astype(MXU_DTYPE))

    imp = _dot_exact_lhs(rsum_ref[...], pc)
    imp_s = _dot_exact_rhs(imp, m_ref[...])
    blk = lax.broadcasted_iota(jnp.int32, imp_s.shape, 1)
    sel = _select_blocks(imp_s, blk, jnp.broadcast_to(qpos, imp_s.shape), 1, n_sel)
    al = _dot(jnp.where(sel, 1.0, 0.0).astype(MXU_DTYPE), e_ref[...])
    n_keys = al.shape[1]
    kpos = lax.broadcasted_iota(jnp.int32, (nrow, n_keys), 1)
    smask = (al > 0.5) & (kpos <= qpos)

    pad_s[2 * sq] = jnp.zeros(pad_s.shape[1:], F32)
    pad_s[2 * sq, 0:sn_ref.shape[1], :] = sn_ref[sq]
    new_rows = pad_s[2 * sq]
    s = jnp.concatenate([_dot(qr, r[0, :kw, :].astype(MXU_DTYPE)) for r in slc_pages]
                        + [_dot_nt(qr, new_rows[:, :kw].astype(MXU_DTYPE))], axis=1)
    ps = softmax(s, smask).astype(MXU_DTYPE)
    o_slc = _dot(ps[:, n_pages * page:], new_rows[:, kw:].astype(MXU_DTYPE))
    for i, r in enumerate(slc_pages):
        o_slc = o_slc + _dot_nt(ps[:, i * page:(i + 1) * page], r[0, kw:, :].astype(MXU_DTYPE))

    wbuf = wst_ref.shape[2]
    pad_s[2 * sq + 1] = jnp.zeros(pad_s.shape[1:], F32)
    pad_s[2 * sq + 1, 0:wn_ref.shape[1], :] = wn_ref[sq]
    new_rows = pad_s[2 * sq + 1]
    s = jnp.concatenate([_dot(qr, wst_ref[sq, :kw, :].astype(MXU_DTYPE)),
                         _dot_nt(qr, new_rows[:, :kw].astype(MXU_DTYPE))], axis=1)
    wpos = (past_len - wbuf) + lax.broadcasted_iota(jnp.int32, s.shape, 1)
    wmask = (wpos <= qpos) & (wpos > qpos - WINDOW) & (wpos >= 0)
    pw = softmax(s, wmask).astype(MXU_DTYPE)
    o_win = (_dot_nt(pw[:, :wbuf], wst_ref[sq, kw:, :].astype(MXU_DTYPE))
             + _dot(pw[:, wbuf:], new_rows[:, kw:].astype(MXU_DTYPE)))

    gt = gt_ref[sq]
    o = gt[:, 0:1] * o_cmp + gt[:, 1:2] * o_slc + gt[:, 2:3] * o_win
    lane_g = lax.broadcasted_iota(jnp.int32, (rows_per_g, kw), 1) // HEAD_DIM
    acc = jnp.zeros((rows_per_g, kw), F32)
    for g in range(N_KV_HEADS):
        acc = acc + jnp.where(lane_g == g, o[g * rows_per_g:(g + 1) * rows_per_g], 0.0)
    o_ref[sq] = acc

    if wout_ref is not None:
        rolled = pltpu.roll(wst_ref[sq], wbuf - t_new, 1)
        wout_ref[sq, :, 0:wbuf - LANES] = rolled[:, 0:wbuf - LANES]
        new_t = pltpu.roll(new_rows.T, LANES - t_new, 1)
        ln = lax.broadcasted_iota(jnp.int32, new_t.shape, 1)
        wout_ref[sq, :, wbuf - LANES:wbuf] = jnp.where(ln >= LANES - t_new, new_t, rolled[:, wbuf - LANES:wbuf])


def _attn_sample(page_table, qn, qr, gates, gk, rsum, m_mat, e_mat, ab_pool, ab_new, cache_slc, slc_new,
                 win_state, win_new, *, n_sel, past_len, t_new, emit_window, seqs_per_step=2):
    nb, n_pages = page_table.shape
    nrow, kw = qn.shape[1], qn.shape[2]
    width = cache_slc.shape[1]
    page = cache_slc.shape[2]
    wbuf = win_state.shape[2]
    spb = seqs_per_step if nb % seqs_per_step == 0 else 1
    per_b = lambda shape: pl.BlockSpec((spb,) + shape, lambda i, pt: (i, 0, 0))

    def page_spec(shape, sq, p):
        return pl.BlockSpec(shape, lambda i, pt: (pt[(i * spb + sq) * n_pages + p], 0, 0))

    in_specs = [per_b((nrow, kw)), per_b((nrow, kw)), per_b((nrow, N_BRANCH)),
                pl.BlockSpec((1, LANES), lambda i, pt: (0, 0)),
                pl.BlockSpec(rsum.shape, lambda i, pt: (0, 0)),
                pl.BlockSpec(m_mat.shape, lambda i, pt: (0, 0)),
                pl.BlockSpec(e_mat.shape, lambda i, pt: (0, 0)),
                per_b(ab_new.shape[1:]), per_b(slc_new.shape[1:]),
                per_b((width, wbuf)), per_b(win_new.shape[1:])]
    in_specs += [page_spec((1,) + ab_pool.shape[1:], sq, p) for sq in range(spb) for p in range(n_pages)]
    in_specs += [page_spec((1, width, page), sq, p) for sq in range(spb) for p in range(n_pages)]
    out_specs = [per_b((nrow // N_KV_HEADS, kw))]
    out_shape = [jax.ShapeDtypeStruct((nb, nrow // N_KV_HEADS, kw), F32)]
    if emit_window:
        out_specs.append(per_b((width, wbuf)))
        out_shape.append(jax.ShapeDtypeStruct((nb, width, wbuf), F32))
    grid_spec = pltpu.PrefetchScalarGridSpec(
        num_scalar_prefetch=1, grid=(nb // spb,), in_specs=in_specs, out_specs=out_specs,
        scratch_shapes=[pltpu.VMEM((2 * spb, page, width), F32)])
    res = pl.pallas_call(
        functools.partial(_attn_sample_kernel, n_pages=n_pages, n_sel=n_sel, past_len=past_len, t_new=t_new,
                          emit_window=emit_window),
        grid_spec=grid_spec, out_shape=out_shape,
        compiler_params=_cparams(1), name="attn_sample",
    )(page_table.reshape(-1), qn, qr, gates, gk, rsum, m_mat, e_mat, ab_new, slc_new, win_state, win_new,
      *([ab_pool] * (spb * n_pages)), *([cache_slc] * (spb * n_pages)))
    return (res[0], res[1]) if emit_window else (res[0], None)


def _rope_tables(pos):
    half = ROPE_DIM // 2
    inv_freq = ROPE_THETA ** (-jnp.arange(half, dtype=F32) / half)
    ang = pos.astype(F32)[:, None] * inv_freq[None, :]
    cos, sin = jnp.cos(ang), jnp.sin(ang)
    n = pos.shape[0]
    z = jnp.zeros((n, HEAD_DIM - ROPE_DIM), F32)
    zh = jnp.zeros((n, half), F32)
    c_head = jnp.concatenate([cos, cos, z + 1.0], axis=1)
    dn_head = jnp.concatenate([zh, sin, z], axis=1)
    up_head = jnp.concatenate([-sin, zh, z], axis=1)
    tile2 = lambda a: jnp.concatenate([a, a], axis=1)
    return tile2(c_head), tile2(dn_head), tile2(up_head)


def _sel_matrices(n_cmp, n_cmp_pad, n_sel, n_keys):
    i = jnp.arange(n_cmp_pad)[:, None]
    j = jnp.arange(LANES)[None, :]
    m = ((i * CMP_STRIDE < j * SEL_BLOCK + SEL_BLOCK) & (i * CMP_STRIDE + CMP_LEN > j * SEL_BLOCK)
         & (i < n_cmp) & (j < n_sel))
    e = (jnp.arange(n_keys)[None, :] // SEL_BLOCK == jnp.arange(LANES)[:, None]) & (jnp.arange(LANES)[:, None] < n_sel)
    return m.astype(MXU_DTYPE), e.astype(MXU_DTYPE)


def _tile2(g):
    return jnp.concatenate([g, g]).reshape(1, LANES).astype(F32)


def kernel(x_prompt, x_sample, cache_kv_cmp, cache_kv_slc, state_kv_win, page_table, norm_mix, norm_ffn,
           ffn_w_gate, ffn_w_up, ffn_w_down, a_w_in, a_ln_g, a_ln_b, a_w_spatial, a_b_spatial, a_w_out,
           kv_norm, kv_w, k_norm_cmp, k_norm_slc, k_norm_win, cmp_pos_k, cmp_pos_v, cmp_w_k, cmp_w_v,
           b_w_in, b_q_norm, b_w_out):
    bf = MXU_DTYPE
    depth = norm_mix.shape[0]
    n_a = a_w_in.shape[0]
    bsz, t_p, d_model = x_prompt.shape
    nb, t_s, _ = x_sample.shape
    n_pages = page_table.shape[1]
    page = cache_kv_cmp.shape[1]
    past_len = n_pages * page
    width = 2 * N_KV_HEADS * HEAD_DIM
    kw = N_KV_HEADS * HEAD_DIM
    n_heads = b_w_out.shape[1] // HEAD_DIM
    rpg = n_heads // N_KV_HEADS
    dq = n_heads * HEAD_DIM
    assert N_KV_HEADS == 4 and rpg == 4 and t_p % CHUNK == 0 and CHUNK % t_s == 0 and t_s <= 8

    wg, wu, wd = ffn_w_gate.astype(bf), ffn_w_up.astype(bf), ffn_w_down.astype(bf)
    a_in, a_out = a_w_in.astype(bf), a_w_out.astype(bf)
    tril = jnp.tril(jnp.ones((CHUNK, CHUNK), bool))
    ws_p = jnp.where(tril[None, None], a_w_spatial, 0.0).astype(bf)
    bs_p = jnp.swapaxes(a_b_spatial, 1, 2)
    rep = CHUNK // t_s
    ws4 = jnp.where(tril[None, None, :t_s, :t_s], a_w_spatial[:, :, :t_s, :t_s], 0.0)
    ws_s = jnp.einsum('ab,lgts->lgatbs', jnp.eye(rep, dtype=F32), ws4).reshape(n_a, A_GROUPS, CHUNK, CHUNK).astype(bf)
    bs_s = jnp.tile(jnp.swapaxes(a_b_spatial[:, :, :t_s], 1, 2), (1, rep, 1))
    perm = jnp.array([(2 * p + side) * rpg + r for p in range(2) for r in range(rpg) for side in range(2)])
    wq = b_w_in[:, :, :dq].reshape(-1, d_model, n_heads, HEAD_DIM)[:, :, perm].reshape(-1, d_model, dq).astype(bf)
    wgate = jnp.pad(b_w_in[:, :, dq:], ((0, 0), (0, 0), (0, LANES - N_BRANCH * n_heads))).astype(bf)
    wo_pair = b_w_out.reshape(-1, n_heads, HEAD_DIM, d_model)[:, perm].reshape(-1, dq, d_model).astype(bf)
    wo_flat = b_w_out.astype(bf)
    kvw = kv_w.astype(bf)
    eye_g = jnp.eye(N_KV_HEADS, dtype=F32)
    w_bd = jnp.stack([jnp.einsum('ab,lde->ladbe', eye_g, w).reshape(CMP_LEN, kw, kw)
                      for w in (cmp_w_k, cmp_w_v)]).astype(bf)
    pos_tab = jnp.concatenate([jnp.tile(cmp_pos_k, (1, N_KV_HEADS)), jnp.tile(cmp_pos_v, (1, N_KV_HEADS))], axis=1)
    gk_cmp, gk_slc, gk_win = _tile2(k_norm_cmp), _tile2(k_norm_slc), _tile2(k_norm_win)

    def dense_a_layers(h, ws_eff, bs_eff, emit_v):
        vs = []
        for l in range(n_a):
            h, v = _gmlp_layer(h, norm_mix[l], a_in[l], a_ln_g[l], a_ln_b[l], ws_eff[l], bs_eff[l], a_out[l],
                               emit_v=emit_v)
            vs.append(v)
            h = _ffn_layer(h, norm_ffn[l], wg[l], wu[l], wd[l])
        return h, vs

    tabs_p = _rope_tables(jnp.arange(t_p))
    h = x_prompt.reshape(bsz * t_p, d_model)
    h, _ = dense_a_layers(h, ws_p, bs_p, False)
    kv_cmp_p, kv_slc_p, kv_win_p, cmp_slabs, slc16, win16 = _kv_project(h, kv_norm, kvw, gk_slc, gk_win, tabs_p,
                                                                        seq_len=t_p)
    ab = _compress_partial(cmp_slabs, pos_tab, w_bd)
    n_sub = t_p // CMP_STRIDE
    kc, vc = _prompt_kc_vc(ab.reshape(bsz, n_sub, 2 * width), gk_cmp)
    n_sel_p = t_p // SEL_BLOCK
    m_p, e_p = _sel_matrices(n_sub - 1, n_sub, n_sel_p, t_p)
    for j in range(depth - n_a):
        l = n_a + j
        qn, qr, gates = _q_project(h, norm_mix[l], wq[j], wgate[j], _tile2(b_q_norm[j]) * ATTN_SCALE, tabs_p)
        o = _attn_prompt(qn, qr, gates, kc, vc, slc16.reshape(bsz, t_p, width), win16.reshape(bsz, t_p, width),
                         e_p.T, m_p.T, n_sel=n_sel_p)
        h = _ffn_layer(h, norm_ffn[l], wg[l], wu[l], wd[l], attn=o, w_o=wo_pair[j])
    y_prompt = h.reshape(bsz, t_p, d_model)
    row5 = lambda a, b_, t_: a.reshape(b_, t_, 2, N_KV_HEADS, HEAD_DIM)
    from_cm = lambda a: jnp.transpose(a.reshape(a.shape[0], 2, N_KV_HEADS, HEAD_DIM, a.shape[2]), (0, 4, 1, 2, 3))
    to_cm = lambda a: jnp.transpose(a, (0, 2, 3, 4, 1)).reshape(a.shape[0], width, a.shape[1])
    w_keep = min(WINDOW, t_p)
    kv_win_prompt = from_cm(kv_win_p[:, :, t_p - w_keep:])

    m_s = nb * t_s
    tabs_s = _rope_tables(past_len + (jnp.arange(nb * t_s) % t_s))
    hs = x_sample.reshape(m_s, d_model)
    hs, vs = dense_a_layers(hs, ws_s, bs_s, True)
    a_v_sample = jnp.stack(vs).reshape(n_a, nb, t_s, -1)
    kv_cmp_s, kv_slc_s, kv_win_s, cmp_slabs_s = _kv_project(hs, kv_norm, kvw, gk_slc, gk_win, tabs_s)
    n_pool = cache_kv_cmp.shape[0]
    sub_pp = page // CMP_STRIDE
    ab_pool = _compress_pages(to_cm(cache_kv_cmp), pos_tab, w_bd).reshape(n_pool, sub_pp, 2 * width)
    t_full = -(-(past_len + t_s) // SEL_BLOCK) * SEL_BLOCK
    n_new_sub = (t_full - past_len) // CMP_STRIDE
    new_cmp = jnp.pad(cmp_slabs_s.reshape(-1, nb, t_s, LANES), ((0, 0), (0, 0), (0, t_full - past_len - t_s), (0, 0)))
    ab_new = _compress_partial(new_cmp.reshape(-1, nb * (t_full - past_len), LANES), pos_tab, w_bd)
    ab_new = jnp.pad(ab_new.reshape(nb, n_new_sub, 2 * width), ((0, 0), (0, 8 - n_new_sub), (0, 0)))
    n_cmp_s = t_full // CMP_STRIDE - 1
    n_sel_s = t_full // SEL_BLOCK
    n_cmp_pad = 256
    n_keys = past_len + page
    m_sm, e_sm = _sel_matrices(n_cmp_s, n_cmp_pad, n_sel_s, n_keys)
    nrow = n_heads * t_s
    ridx = jnp.arange(nrow)
    rsum = ((ridx[:, None] // (rpg * t_s) == ridx[None, :] // (rpg * t_s))
            & (ridx[:, None] % t_s == ridx[None, :] % t_s)).astype(bf)
    pad8 = lambda a: jnp.pad(a.reshape(nb, t_s, width), ((0, 0), (0, 8 - t_s), (0, 0)))
    slc_new, win_new = pad8(kv_slc_s), pad8(kv_win_s)
    inv_perm = jnp.argsort(perm)

    def pad_q(q):
        q = q.reshape(nb, t_s, n_heads, HEAD_DIM)[:, :, inv_perm].reshape(nb, t_s, N_KV_HEADS, rpg, HEAD_DIM)
        q = jnp.transpose(q, (0, 2, 3, 1, 4))
        q = jnp.einsum('bgrtd,gk->bgrtkd', q, jnp.eye(N_KV_HEADS, dtype=q.dtype))
        return q.reshape(nb, nrow, kw)

    slc_cm, win_cm = to_cm(cache_kv_slc), to_cm(state_kv_win)
    kv_win_sample = None
    for j in range(depth - n_a):
        l = n_a + j
        qn, qr, gates = _q_project(hs, norm_mix[l], wq[j], wgate[j], _tile2(b_q_norm[j]) * ATTN_SCALE, tabs_s)
        g3 = gates[:N_BRANCH * n_heads].T.reshape(nb, t_s, N_KV_HEADS, rpg, N_BRANCH)
        g3 = jnp.transpose(g3, (0, 2, 3, 1, 4)).reshape(nb, nrow, N_BRANCH)
        o, w_out = _attn_sample(page_table, pad_q(qn), pad_q(qr), g3, gk_cmp, rsum, m_sm, e_sm,
                                ab_pool, ab_new, slc_cm, slc_new, win_cm, win_new,
                                n_sel=n_sel_s, past_len=past_len, t_new=t_s, emit_window=j == 0)
        if j == 0:
            kv_win_sample = from_cm(w_out)
        o = jnp.transpose(o.reshape(nb, rpg, t_s, N_KV_HEADS, HEAD_DIM), (0, 2, 3, 1, 4)).reshape(m_s, dq)
        hs = _ffn_layer(hs, norm_ffn[l], wg[l], wu[l], wd[l], attn=o.astype(bf), w_o=wo_flat[j])
    y_sample = hs.reshape(nb, t_s, d_model)

    return (y_prompt, y_sample, from_cm(kv_cmp_p), row5(kv_cmp_s, nb, t_s),
            from_cm(kv_slc_p), row5(kv_slc_s, nb, t_s), kv_win_prompt, kv_win_sample, a_v_sample)
```

```python
import functools

import jax
import jax.numpy as jnp
from jax import lax
from jax.experimental import pallas as pl
from jax.experimental.pallas import tpu as pltpu

CHUNK = 128
A_GROUPS = 8
HEAD_DIM = 64
N_KV_HEADS = 4
ROPE_DIM = HEAD_DIM // 4
ROPE_THETA = 500000.0
CMP_STRIDE = 16
CMP_LEN = 2 * CMP_STRIDE
SEL_BLOCK = 64
SEL_TOPK = 16
WINDOW = 512
N_BRANCH = 3
EPS = 1e-6
ATTN_SCALE = HEAD_DIM ** -0.5

LANES = 128
PAGE_SUB_PITCH = 24
VMEM_LIMIT_BYTES = 56 * 1024 * 1024
MXU_DTYPE = jnp.bfloat16
NEG = -1e30
F32 = jnp.float32


def _cparams(n_grid):
    return pltpu.CompilerParams(dimension_semantics=("arbitrary",) * n_grid,
                                vmem_limit_bytes=VMEM_LIMIT_BYTES)


def _const_spec(shape):
    nd = len(shape)
    return pl.BlockSpec(shape, lambda *_: (0,) * nd, pipeline_mode=pl.Buffered(1))


def _dot(a, b):
    return jnp.dot(a, b, preferred_element_type=F32)


def _dot_nt(a, b):
    return lax.dot_general(a, b, (((1,), (1,)), ((), ())), preferred_element_type=F32)


def _split3(x):
    hi = x.astype(MXU_DTYPE)
    r1 = x - hi.astype(F32)
    mid = r1.astype(MXU_DTYPE)
    lo = (r1 - mid.astype(F32)).astype(MXU_DTYPE)
    return hi, mid, lo


def _dot_exact_lhs(a, x):
    hi, mid, lo = _split3(x)
    return _dot(a, hi) + _dot(a, mid) + _dot(a, lo)


def _dot_exact_rhs(x, b):
    hi, mid, lo = _split3(x)
    return _dot(hi, b) + _dot(mid, b) + _dot(lo, b)


def _rms(x, g):
    return x * lax.rsqrt(jnp.mean(x * x, axis=-1, keepdims=True) + EPS) * g


def _headnorm128(x, g):
    lo = lax.broadcasted_iota(jnp.int32, x.shape, 1) < HEAD_DIM
    sq = x * x
    s_lo = jnp.sum(jnp.where(lo, sq, 0.0), axis=-1, keepdims=True)
    s_hi = jnp.sum(jnp.where(lo, 0.0, sq), axis=-1, keepdims=True)
    ms = jnp.where(lo, s_lo, s_hi) * (1.0 / HEAD_DIM)
    return x * lax.rsqrt(ms + EPS) * g


def _rope128(x, cos, s_dn, s_up):
    half = ROPE_DIM // 2
    return x * cos + pltpu.roll(x, half, 1) * s_dn + pltpu.roll(x, LANES - half, 1) * s_up


def _gmlp_kernel(h_ref, g_ref, win_ref, lng_ref, lnb_ref, ws_ref, bs_ref, wout_ref, gf_ref, wg_ref, wu_ref,
                 wd_ref, *rest, emit_v):
    if emit_v:
        o_ref, v_ref, gated = rest
    else:
        o_ref, gated = rest
    tm = h_ref.shape[0]
    d_a = lng_ref.shape[1]
    dg = d_a // A_GROUPS
    x = h_ref[...]
    xn = _rms(x, g_ref[...]).astype(MXU_DTYPE)
    v = _dot(xn, win_ref[:, d_a:])
    u = _dot(xn, win_ref[:, :d_a])
    v = jax.nn.gelu(v)
    vc = v - jnp.mean(v, axis=-1, keepdims=True)
    v = vc * lax.rsqrt(jnp.mean(vc * vc, axis=-1, keepdims=True) + EPS) * lng_ref[...] + lnb_ref[...]
    if emit_v:
        v_ref[...] = v
    vb = v.astype(MXU_DTYPE)
    for c in range(tm // CHUNK):
        r0, r1 = c * CHUNK, (c + 1) * CHUNK
        for g in range(A_GROUPS):
            c0, c1 = g * dg, (g + 1) * dg
            sv = _dot(ws_ref[g], vb[r0:r1, c0:c1]) + bs_ref[:, g:g + 1]
            gated[r0:r1, c0:c1] = (jax.nn.gelu(u[r0:r1, c0:c1]) * sv).astype(MXU_DTYPE)
    x = x + _dot(gated[...], wout_ref[...])
    xn = _rms(x, gf_ref[...]).astype(MXU_DTYPE)
    a, b = _dot(xn, wg_ref[...]), _dot(xn, wu_ref[...])
    mid = (jax.nn.silu(a) * b).astype(MXU_DTYPE)
    o_ref[...] = x + _dot(mid, wd_ref[...])


def _a_layer(h, g, w_in, ln_g, ln_b, ws_eff, bs_eff, w_out, g_ffn, wg, wu, wd, *, emit_v, tm=256):
    m, d = h.shape
    tm = min(tm, m)
    d_a = ln_g.shape[-1]
    row = pl.BlockSpec((tm, d), lambda i: (i, 0))
    out_shape = [jax.ShapeDtypeStruct((m, d), F32)]
    out_specs = [row]
    if emit_v:
        out_shape.append(jax.ShapeDtypeStruct((m, d_a), F32))
        out_specs.append(pl.BlockSpec((tm, d_a), lambda i: (i, 0)))
    res = pl.pallas_call(
        functools.partial(_gmlp_kernel, emit_v=emit_v),
        grid=(m // tm,),
        in_specs=[row, _const_spec((1, d)), _const_spec(w_in.shape), _const_spec((1, d_a)),
                  _const_spec((1, d_a)), _const_spec(ws_eff.shape), _const_spec(bs_eff.shape),
                  _const_spec(w_out.shape), _const_spec((1, d)), _const_spec(wg.shape), _const_spec(wu.shape),
                  _const_spec(wd.shape)],
        out_specs=out_specs, out_shape=out_shape,
        scratch_shapes=[pltpu.VMEM((tm, d_a), MXU_DTYPE)],
        compiler_params=_cparams(1), name="a_layer",
    )(h, g.reshape(1, d), w_in, ln_g.reshape(1, d_a), ln_b.reshape(1, d_a), ws_eff, bs_eff, w_out,
      g_ffn.reshape(1, d), wg, wu, wd)
    return (res[0], res[1]) if emit_v else (res[0], None)


def _ffn_kernel(*refs, with_proj):
    if with_proj:
        h_ref, a_ref, wo_ref, g_ref, wg_ref, wu_ref, wd_ref, o_ref = refs
        x = h_ref[...] + _dot(a_ref[...], wo_ref[...])
    else:
        h_ref, g_ref, wg_ref, wu_ref, wd_ref, o_ref = refs
        x = h_ref[...]
    xn = _rms(x, g_ref[...]).astype(MXU_DTYPE)
    a, b = _dot(xn, wg_ref[...]), _dot(xn, wu_ref[...])
    mid = (jax.nn.silu(a) * b).astype(MXU_DTYPE)
    o_ref[...] = x + _dot(mid, wd_ref[...])


def _ffn_layer(h, g, wg, wu, wd, attn=None, w_o=None, *, tm=512):
    m, d = h.shape
    tm = min(tm, m)
    row = pl.BlockSpec((tm, d), lambda i: (i, 0))
    with_proj = attn is not None
    args, specs = [h], [row]
    if with_proj:
        args += [attn, w_o]
        specs += [pl.BlockSpec((tm, attn.shape[1]), lambda i: (i, 0)), _const_spec(w_o.shape)]
    args += [g.reshape(1, d), wg, wu, wd]
    specs += [_const_spec((1, d)), _const_spec(wg.shape), _const_spec(wu.shape), _const_spec(wd.shape)]
    return pl.pallas_call(
        functools.partial(_ffn_kernel, with_proj=with_proj),
        grid=(m // tm,), in_specs=specs, out_specs=row,
        out_shape=jax.ShapeDtypeStruct((m, d), F32),
        compiler_params=_cparams(1), name="ffn_layer",
    )(*args)


def _kvproj_kernel(h_ref, g_ref, w_ref, gs_ref, gw_ref, cos_ref, sdn_ref, sup_ref,
                   cmp_ref, slc_ref, win_ref, slabs_ref, *bf_refs, channel_major):
    x = h_ref[...]
    xn = _rms(x, g_ref[...]).astype(MXU_DTYPE)
    kv = _dot(xn, w_ref[...])
    width = 2 * N_KV_HEADS * HEAD_DIM
    kw = N_KV_HEADS * HEAD_DIM
    cos, sdn, sup = cos_ref[...], sdn_ref[...], sup_ref[...]

    def emit(o_ref, c, tile):
        if channel_major:
            o_ref[0, c * LANES:(c + 1) * LANES, :] = tile.T
        else:
            o_ref[:, c * LANES:(c + 1) * LANES] = tile

    for c in range(width // LANES):
        tile = kv[:, c * LANES:(c + 1) * LANES]
        emit(cmp_ref, c, tile)
        slabs_ref[c] = tile
    for bi, (base, gk_ref, o_ref) in enumerate(((width, gs_ref, slc_ref), (2 * width, gw_ref, win_ref))):
        for c in range(width // LANES):
            tile = kv[:, base + c * LANES: base + (c + 1) * LANES]
            if c < kw // LANES:
                tile = _rope128(_headnorm128(tile, gk_ref[...]), cos, sdn, sup)
            emit(o_ref, c, tile)
            if channel_major:
                bf_refs[bi][:, c * LANES:(c + 1) * LANES] = tile.astype(MXU_DTYPE)


def _kv_project(h, kv_norm, kv_w, gk_slc, gk_win, tabs, *, seq_len=None, tm=512):
    m, d = h.shape
    tm = min(tm, m)
    width = 2 * N_KV_HEADS * HEAD_DIM
    n_tab = tabs[0].shape[0] // tm
    row = pl.BlockSpec((tm, d), lambda i: (i, 0))
    tab = pl.BlockSpec((tm, LANES), lambda i: (i % n_tab, 0))
    orow = pl.BlockSpec((tm, width), lambda i: (i, 0))
    slab = pl.BlockSpec((width // LANES, tm, LANES), lambda i: (0, i, 0))
    slab_shape = jax.ShapeDtypeStruct((width // LANES, m, LANES), F32)
    if seq_len is not None:
        nblk = seq_len // tm
        ocm = pl.BlockSpec((1, width, tm), lambda i: (i // nblk, 0, i % nblk))
        out_specs = [ocm] * 3 + [slab, orow, orow]
        out_shape = ([jax.ShapeDtypeStruct((m // seq_len, width, seq_len), F32)] * 3 + [slab_shape]
                     + [jax.ShapeDtypeStruct((m, width), MXU_DTYPE)] * 2)
    else:
        out_specs = [orow] * 3 + [slab]
        out_shape = [jax.ShapeDtypeStruct((m, width), F32)] * 3 + [slab_shape]
    return pl.pallas_call(
        functools.partial(_kvproj_kernel, channel_major=seq_len is not None), grid=(m // tm,),
        in_specs=[row, _const_spec((1, d)), _const_spec(kv_w.shape), _const_spec((1, LANES)),
                  _const_spec((1, LANES)), tab, tab, tab],
        out_specs=out_specs, out_shape=out_shape,
        compiler_params=_cparams(1), name="kv_project",
    )(h, kv_norm.reshape(1, d), kv_w, gk_slc, gk_win, *tabs)


def _qproj_kernel(h_ref, g_ref, wq_ref, wgt_ref, gq_ref, cos_ref, sdn_ref, sup_ref,
                  qn_ref, qr_ref, gates_ref):
    x = h_ref[...]
    xn = _rms(x, g_ref[...]).astype(MXU_DTYPE)
    q = _dot(xn, wq_ref[...])
    gates_ref[...] = jax.nn.sigmoid(_dot(xn, wgt_ref[...])).T
    cos, sdn, sup = cos_ref[...], sdn_ref[...], sup_ref[...]
    for c in range(q.shape[1] // LANES):
        qn = _headnorm128(q[:, c * LANES:(c + 1) * LANES], gq_ref[...])
        qn_ref[:, c * LANES:(c + 1) * LANES] = qn.astype(qn_ref.dtype)
        qr_ref[:, c * LANES:(c + 1) * LANES] = _rope128(qn, cos, sdn, sup).astype(qr_ref.dtype)


def _q_project(h, g, w_q, w_gate, gq, tabs, *, tm=512):
    m, d = h.shape
    tm = min(tm, m)
    dq = w_q.shape[1]
    n_tab = tabs[0].shape[0] // tm
    row = pl.BlockSpec((tm, d), lambda i: (i, 0))
    tab = pl.BlockSpec((tm, LANES), lambda i: (i % n_tab, 0))
    qrow = pl.BlockSpec((tm, dq), lambda i: (i, 0))
    return pl.pallas_call(
        _qproj_kernel, grid=(m // tm,),
        in_specs=[row, _const_spec((1, d)), _const_spec(w_q.shape), _const_spec(w_gate.shape),
                  _const_spec((1, LANES)), tab, tab, tab],
        out_specs=[qrow, qrow, pl.BlockSpec((LANES, tm), lambda i: (0, i))],
        out_shape=[jax.ShapeDtypeStruct((m, dq), MXU_DTYPE)] * 2 + [jax.ShapeDtypeStruct((LANES, m), F32)],
        compiler_params=_cparams(1), name="q_project",
    )(h, g.reshape(1, d), w_q, w_gate, gq, *tabs)


def _compress_from_slabs(x_ref, pos_ref, w_ref, o_ref, pitch=CMP_STRIDE, row0=0, mb=None, between=None):
    width = 2 * N_KV_HEADS * HEAD_DIM
    kw = N_KV_HEADS * HEAD_DIM
    mb = o_ref.shape[0] if mb is None else mb
    spk = kw // LANES
    for kv in range(2):
        acc_a = jnp.zeros((mb, kw), F32)
        acc_b = jnp.zeros((mb, kw), F32)
        for l in range(CMP_STRIDE):
            z = jnp.concatenate([x_ref[kv * spk + j, pl.ds(l, mb, stride=pitch), :] for j in range(spk)], axis=1)
            pa = pos_ref[l:l + 1, kv * kw:(kv + 1) * kw]
            pb = pos_ref[CMP_STRIDE + l:CMP_STRIDE + l + 1, kv * kw:(kv + 1) * kw]
            acc_a = acc_a + _dot((z + pa).astype(MXU_DTYPE), w_ref[kv, l])
            acc_b = acc_b + _dot((z + pb).astype(MXU_DTYPE), w_ref[kv, CMP_STRIDE + l])
            if between is not None:
                between(kv * CMP_STRIDE + l)
        o_ref[row0:row0 + mb, kv * kw:(kv + 1) * kw] = acc_a
        o_ref[row0:row0 + mb, width + kv * kw: width + (kv + 1) * kw] = acc_b


def _compress_pages_kernel(x_ref, pos_ref, w_ref, o_ref, xt_a, xt_b):
    n_pg, width, page = x_ref.shape
    sub_pp = page // CMP_STRIDE
    half = n_pg // 2
    mbh = half * sub_pp

    def transpose_page(xt, pg_src, pg_dst):
        r0 = pg_dst * (sub_pp * PAGE_SUB_PITCH)
        if not isinstance(r0, int):
            r0 = pl.multiple_of(r0, 8)
        for c in range(width // LANES):
            rows = x_ref[pg_src, c * LANES:(c + 1) * LANES, :].T
            for m in range(sub_pp):
                xt[c, pl.ds(r0 + m * PAGE_SUB_PITCH, CMP_STRIDE), :] = rows[m * CMP_STRIDE:(m + 1) * CMP_STRIDE]

    def body(pg, carry):
        transpose_page(xt_a, pg, pg)
        return carry

    lax.fori_loop(0, half, body, 0, unroll=4)
    steps = 2 * CMP_STRIDE
    per_step = -(-half // steps)

    def between(i):
        for j in range(i * per_step, min((i + 1) * per_step, half)):
            transpose_page(xt_b, half + j, j)

    _compress_from_slabs(xt_a, pos_ref, w_ref, o_ref, pitch=PAGE_SUB_PITCH, row0=0, mb=mbh, between=between)
    _compress_from_slabs(xt_b, pos_ref, w_ref, o_ref, pitch=PAGE_SUB_PITCH, row0=mbh, mb=mbh)


def _compress_partial(slabs, pos_tab, w_bd, *, mb=256):
    n_slab, n_rows, _ = slabs.shape
    n_sub = n_rows // CMP_STRIDE
    mb = min(mb, n_sub)
    width = 2 * N_KV_HEADS * HEAD_DIM
    return pl.pallas_call(
        _compress_from_slabs, grid=(n_sub // mb,),
        in_specs=[pl.BlockSpec((n_slab, mb * CMP_STRIDE, LANES), lambda i: (0, i, 0)),
                  _const_spec(pos_tab.shape), _const_spec(w_bd.shape)],
        out_specs=pl.BlockSpec((mb, 2 * width), lambda i: (i, 0)),
        out_shape=jax.ShapeDtypeStruct((n_sub, 2 * width), F32),
        compiler_params=_cparams(1), name="compress_partial",
    )(slabs, pos_tab, w_bd)


def _compress_pages(pages_cm, pos_tab, w_bd, *, pages_per_step=32):
    n_pool, width, page = pages_cm.shape
    pps = min(pages_per_step, n_pool)
    assert n_pool % pps == 0 and pps % 2 == 0
    mb = pps * page // CMP_STRIDE
    return pl.pallas_call(
        _compress_pages_kernel, grid=(n_pool // pps,),
        in_specs=[pl.BlockSpec((pps, width, page), lambda i: (i, 0, 0)),
                  _const_spec(pos_tab.shape), _const_spec(w_bd.shape)],
        out_specs=pl.BlockSpec((mb, 2 * width), lambda i: (i, 0)),
        out_shape=jax.ShapeDtypeStruct((n_pool * page // CMP_STRIDE, 2 * width), F32),
        scratch_shapes=[pltpu.VMEM((width // LANES, mb // 2 * PAGE_SUB_PITCH, LANES), F32)] * 2,
        compiler_params=_cparams(1), name="compress_pages",
    )(pages_cm, pos_tab, w_bd)


def _assemble_kc_vc(ab, gk):
    width = 2 * N_KV_HEADS * HEAD_DIM
    kw = N_KV_HEADS * HEAD_DIM
    n = ab.shape[0]
    kcv = ab[:, :width] + pltpu.roll(ab[:, width:], n - 1, 0)
    kc = jnp.concatenate([_headnorm128(kcv[:, c * LANES:(c + 1) * LANES], gk) for c in range(kw // LANES)],
                         axis=1)
    return kc, kcv[:, kw:width]


def _kcvc_kernel(ab_ref, gk_ref, kc_ref, vc_ref):
    kc, vc = _assemble_kc_vc(ab_ref[0], gk_ref[...])
    kc_ref[0] = kc.astype(kc_ref.dtype)
    vc_ref[0] = vc.astype(vc_ref.dtype)


def _prompt_kc_vc(ab, gk):
    b, n, w = ab.shape
    kw = N_KV_HEADS * HEAD_DIM
    spec = pl.BlockSpec((1, n, kw), lambda i: (i, 0, 0))
    return pl.pallas_call(
        _kcvc_kernel, grid=(b,),
        in_specs=[pl.BlockSpec((1, n, w), lambda i: (i, 0, 0)), _const_spec((1, LANES))],
        out_specs=[spec, spec], out_shape=[jax.ShapeDtypeStruct((b, n, kw), MXU_DTYPE)] * 2,
        compiler_params=_cparams(1), name="prompt_kc_vc",
    )(ab, gk)


def _select_blocks(imp, blk, qpos, axis, n_sel):
    cur = qpos // SEL_BLOCK
    valid = blk <= cur
    forced = (blk == 0) | (blk == cur) | (blk == cur - 1)
    sc = jnp.where(forced, jnp.inf, jnp.where(valid, imp, -jnp.inf))
    cnt = jnp.zeros(sc.shape, jnp.int32)
    for j in range(n_sel):
        sj = lax.slice_in_dim(sc, j, j + 1, axis=axis)
        beats = (sj > sc) | ((sj == sc) & (blk > j))
        cnt = cnt + beats.astype(jnp.int32)
    return (cnt < min(SEL_TOPK, n_sel)) & (sc > -jnp.inf) & (blk < n_sel)


def _dot_tn(a, b):
    return lax.dot_general(a, b, (((0,), (0,)), ((), ())), preferred_element_type=F32)


def _attn_prompt_kernel(qn_ref, qr_ref, gt_ref, kc_ref, vc_ref, slc_ref, win_ref, et_ref, mt_ref, o_ref,
                        *, n_sel, tk, cs):
    tq = qn_ref.shape[0]
    t_len = slc_ref.shape[1]
    kw = N_KV_HEADS * HEAD_DIM
    rpg = 4
    ns = 2 * rpg
    qb = pl.program_id(1)
    lo = lax.broadcasted_iota(jnp.int32, (tq, LANES), 1) < HEAD_DIM
    top = lax.broadcasted_iota(jnp.int32, (LANES, tq), 0) < HEAD_DIM
    wlen = min(WINDOW + tq, t_len)

    def q_of(shape):
        return qb * tq + lax.broadcasted_iota(jnp.int32, shape, 1)

    def parts_of(ref, p):
        parts = []
        for side in range(2):
            keep = lo if side == 0 else jnp.logical_not(lo)
            for r in range(rpg):
                c = p * rpg + r
                t = ref[:, c * LANES:(c + 1) * LANES]
                parts.append(jnp.where(keep, t, jnp.zeros_like(t)))
        return parts

    def slabs(x0, x1):
        return jnp.concatenate([x0] * rpg + [x1] * rpg, axis=1)

    def flash(kv_ref, q_chunks, n_full, tkk):
        n_ch = len(q_chunks)

        def tile(kt, carry, causal):
            ms, ls, accs = carry
            k0 = pl.multiple_of(kt * tkk, tkk)
            onehot = et_ref[pl.ds(k0, tkk), :]
            if causal:
                ok = (k0 + lax.broadcasted_iota(jnp.int32, (tkk, tq), 0)) <= q_of((tkk, tq))
                cb = jnp.concatenate([jnp.where(ok, 0.0, NEG)] * cs, axis=1)
            new_m, new_l, new_acc = [], [], []
            scores = []
            for p, qc in q_chunks:
                k = jnp.concatenate([kv_ref[0, pl.ds(k0, tkk), p * LANES:(p + 1) * LANES], onehot], axis=1)
                scores.append(_dot_nt(k, qc))
            for ci, (p, qc) in enumerate(q_chunks):
                v = kv_ref[0, pl.ds(k0, tkk), kw + p * LANES: kw + (p + 1) * LANES]
                s = scores[ci] + cb if causal else scores[ci]
                m_new = jnp.maximum(ms[ci], jnp.max(s, axis=0, keepdims=True))
                alpha = jnp.exp(ms[ci] - m_new)
                pe = jnp.exp(s - m_new)
                new_m.append(m_new)
                new_l.append(alpha * ls[ci] + jnp.sum(pe, axis=0, keepdims=True))
                new_acc.append(alpha * accs[ci] + _dot_tn(v, pe.astype(MXU_DTYPE)))
            return tuple(new_m), tuple(new_l), tuple(new_acc)

        init = (tuple(jnp.full((1, cs * tq), NEG, F32) for _ in range(n_ch)),
                tuple(jnp.zeros((1, cs * tq), F32) for _ in range(n_ch)),
                tuple(jnp.zeros((LANES, cs * tq), F32) for _ in range(n_ch)))
        carry = lax.fori_loop(0, n_full, functools.partial(tile, causal=False), init)
        _, ls, accs = tile(n_full, carry, True)
        outs = [a / jnp.maximum(l, 1e-30) for a, l in zip(accs, ls)]
        per_pair = n_ch // 2
        return [jnp.concatenate(outs[p * per_pair:(p + 1) * per_pair], axis=1) for p in range(2)]

    qr_parts = [parts_of(qr_ref, p) for p in range(2)]
    k0 = pl.multiple_of(jnp.maximum(qb * tq + tq - wlen, 0), tq)
    kpos = k0 + lax.broadcasted_iota(jnp.int32, (wlen, tq), 0)
    qp = q_of((wlen, tq))
    wb = jnp.where((kpos <= qp) & (kpos > qp - WINDOW), 0.0, NEG)
    o_win = []
    w_scores = [_dot_nt(win_ref[0, pl.ds(k0, wlen), p * LANES:(p + 1) * LANES], jnp.concatenate(qr_parts[p], axis=0))
                for p in range(2)]
    for p in range(2):
        v = win_ref[0, pl.ds(k0, wlen), kw + p * LANES: kw + (p + 1) * LANES]
        s = w_scores[p] + slabs(wb, wb)
        pe = jnp.exp(s - jnp.max(s, axis=0, keepdims=True))
        o_win.append(_dot_tn(v, pe.astype(MXU_DTYPE)) / jnp.maximum(jnp.sum(pe, axis=0, keepdims=True), 1e-30))

    o_cmp, sel_pen = [], []
    for p in range(2):
        qn_st = jnp.concatenate(parts_of(qn_ref, p), axis=0)
        s = _dot_nt(kc_ref[0, :, p * LANES:(p + 1) * LANES], qn_st)
        n_cb = s.shape[0]
        cm1 = (lax.broadcasted_iota(jnp.int32, (n_cb, tq), 0) * CMP_STRIDE + (CMP_LEN - 1)) <= q_of((n_cb, tq))
        cmask = slabs(cm1, cm1)
        s = jnp.where(cmask, s, -jnp.inf)
        mx = jnp.max(s, axis=0, keepdims=True)
        mx = jnp.where(mx == -jnp.inf, 0.0, mx)
        e = jnp.where(cmask, jnp.exp(s - mx), 0.0)
        pr = e / jnp.maximum(jnp.sum(e, axis=0, keepdims=True), 1e-30)
        o_cmp.append(_dot_tn(vc_ref[0, :, p * LANES:(p + 1) * LANES], pr.astype(MXU_DTYPE)))

        for side in range(2):
            imp = pr[:, side * rpg * tq:(side * rpg + 1) * tq]
            for r in range(1, rpg):
                imp = imp + pr[:, (side * rpg + r) * tq:(side * rpg + r + 1) * tq]
            imp_t = _dot_exact_lhs(mt_ref[...], imp)[:n_sel]
            blk = lax.broadcasted_iota(jnp.int32, (n_sel, tq), 0)
            sel = _select_blocks(imp_t, blk, q_of((n_sel, tq)), 0, n_sel)
            pen = jnp.concatenate([jnp.where(sel, 0.0, NEG).astype(F32),
                                   jnp.full((LANES - n_sel, tq), NEG, F32)], axis=0)
            sel_pen.append(pen.T.astype(MXU_DTYPE))

    q_chunks = [(p, jnp.concatenate([jnp.concatenate([qr_parts[p][j + jj], sel_pen[2 * p + (j + jj) // rpg]], axis=1)
                                     for jj in range(cs)], axis=0))
                for p in range(2) for j in range(0, ns, cs)]
    o_slc = flash(slc_ref, q_chunks, (qb * tq + tq - 1) // tk, tk)

    gt = gt_ref[...]
    for p in range(2):
        for r in range(rpg):
            c = p * rpg + r
            h_lo = (2 * p) * rpg + r
            h_hi = (2 * p + 1) * rpg + r

            def gate(br):
                return jnp.where(top, gt[h_lo * N_BRANCH + br: h_lo * N_BRANCH + br + 1, :],
                                 gt[h_hi * N_BRANCH + br: h_hi * N_BRANCH + br + 1, :])

            def pick(x):
                return jnp.where(top, x[:, r * tq:(r + 1) * tq], x[:, (rpg + r) * tq:(rpg + r + 1) * tq])

            o = gate(0) * pick(o_cmp[p]) + gate(1) * pick(o_slc[p]) + gate(2) * pick(o_win[p])
            o_ref[:, c * LANES:(c + 1) * LANES] = o.T.astype(o_ref.dtype)


def _attn_prompt(qn, qr, gates_t, kc, vc, slc16, win16, e_t, mt, *, n_sel, tq=128, tk=512, cs=8):
    b, t, w = slc16.shape
    nq = t // tq
    tk = min(tk, t)
    dq = qn.shape[1]
    n_cb = kc.shape[1]
    assert tq == LANES and t % tk == 0 and tk % tq == 0
    qspec = pl.BlockSpec((tq, dq), lambda i, j: (i * nq + j, 0))
    seq = lambda shape: pl.BlockSpec(shape, lambda i, j: (i, 0, 0))
    return pl.pallas_call(
        functools.partial(_attn_prompt_kernel, n_sel=n_sel, tk=tk, cs=cs),
        grid=(b, nq),
        in_specs=[qspec, qspec, pl.BlockSpec((LANES, tq), lambda i, j: (0, i * nq + j)),
                  seq((1, n_cb, kc.shape[2])), seq((1, n_cb, vc.shape[2])),
                  seq((1, t, w)), seq((1, t, w)), _const_spec(e_t.shape), _const_spec(mt.shape)],
        out_specs=qspec, out_shape=jax.ShapeDtypeStruct((b * t, dq), MXU_DTYPE),
        compiler_params=_cparams(2), name="attn_prompt",
    )(qn, qr, gates_t, kc, vc, slc16, win16, e_t, mt)


def _attn_sample_kernel(pt_ref, qn_ref, qr_ref, gt_ref, gk_ref, rsum_ref, m_ref, e_ref, abn_ref, sn_ref,
                        wst_ref, wn_ref, *rest, n_pages, n_sel, past_len, t_new, emit_window):
    spb = qn_ref.shape[0]
    if emit_window:
        o_ref, wout_ref, pad_s = rest[2 * spb * n_pages:]
    else:
        (o_ref, pad_s), wout_ref = rest[2 * spb * n_pages:], None
    for s in range(spb):
        _attn_sample_one(s, qn_ref, qr_ref, gt_ref, gk_ref, rsum_ref, m_ref, e_ref, abn_ref, sn_ref, wst_ref, wn_ref,
                         rest[s * n_pages:(s + 1) * n_pages],
                         rest[(spb + s) * n_pages:(spb + s + 1) * n_pages], o_ref, wout_ref, pad_s,
                         n_pages=n_pages, n_sel=n_sel, past_len=past_len, t_new=t_new)


def _attn_sample_one(sq, qn_ref, qr_ref, gt_ref, gk_ref, rsum_ref, m_ref, e_ref, abn_ref, sn_ref, wst_ref, wn_ref,
                     ab_pages, slc_pages, o_ref, wout_ref, pad_s, *, n_pages, n_sel, past_len, t_new):
    kw = N_KV_HEADS * HEAD_DIM
    page = slc_pages[0].shape[2]
    nrow = qn_ref.shape[1]
    rows_per_g = nrow // N_KV_HEADS
    qn = qn_ref[sq]
    qr = qr_ref[sq]
    t_of_row = lax.broadcasted_iota(jnp.int32, (nrow, 1), 0) % t_new
    qpos = past_len + t_of_row

    def softmax(s, mask):
        s = jnp.where(mask, s, -jnp.inf)
        mx = jnp.max(s, axis=-1, keepdims=True)
        mx = jnp.where(mx == -jnp.inf, 0.0, mx)
        e = jnp.where(mask, jnp.exp(s - mx), 0.0)
        return e / jnp.maximum(jnp.sum(e, axis=-1, keepdims=True), 1e-30)

    n_ab = m_ref.shape[0]
    pieces = [r[0] for r in ab_pages] + [abn_ref[sq]]
    n_have = sum(x.shape[0] for x in pieces)
    pieces.append(jnp.zeros((n_ab - n_have, pieces[0].shape[1]), F32))
    kc, vc = _assemble_kc_vc(jnp.concatenate(pieces, axis=0), gk_ref[...])
    blk_c = lax.broadcasted_iota(jnp.int32, (nrow, n_ab), 1)
    cmask = (blk_c * CMP_STRIDE + (CMP_LEN - 1)) <= qpos
    pc = softmax(_dot_nt(qn, kc.astype(MXU_DTYPE)), cmask)
    o_cmp = _dot(pc.astype(MXU_DTYPE), vc.astype(MXU_DTYPE))

    imp = _dot_exact_lhs(rsum_ref[...], pc)
    imp_s = _dot_exact_rhs(imp, m_ref[...])
    blk = lax.broadcasted_iota(jnp.int32, imp_s.shape, 1)
    sel = _select_blocks(imp_s, blk, jnp.broadcast_to(qpos, imp_s.shape), 1, n_sel)
    al = _dot(jnp.where(sel, 1.0, 0.0).astype(MXU_DTYPE), e_ref[...])
    n_keys = al.shape[1]
    kpos = lax.broadcasted_iota(jnp.int32, (nrow, n_keys), 1)
    smask = (al > 0.5) & (kpos <= qpos)

    pad_s[2 * sq] = jnp.zeros(pad_s.shape[1:], F32)
    pad_s[2 * sq, 0:sn_ref.shape[1], :] = sn_ref[sq]
    new_rows = pad_s[2 * sq]
    s = jnp.concatenate([_dot(qr, r[0, :kw, :].astype(MXU_DTYPE)) for r in slc_pages]
                        + [_dot_nt(qr, new_rows[:, :kw].astype(MXU_DTYPE))], axis=1)
    ps = softmax(s, smask).astype(MXU_DTYPE)
    o_slc = _dot(ps[:, n_pages * page:], new_rows[:, kw:].astype(MXU_DTYPE))
    for i, r in enumerate(slc_pages):
        o_slc = o_slc + _dot_nt(ps[:, i * page:(i + 1) * page], r[0, kw:, :].astype(MXU_DTYPE))

    wbuf = wst_ref.shape[2]
    pad_s[2 * sq + 1] = jnp.zeros(pad_s.shape[1:], F32)
    pad_s[2 * sq + 1, 0:wn_ref.shape[1], :] = wn_ref[sq]
    new_rows = pad_s[2 * sq + 1]
    s = jnp.concatenate([_dot(qr, wst_ref[sq, :kw, :].astype(MXU_DTYPE)),
                         _dot_nt(qr, new_rows[:, :kw].astype(MXU_DTYPE))], axis=1)
    wpos = (past_len - wbuf) + lax.broadcasted_iota(jnp.int32, s.shape, 1)
    wmask = (wpos <= qpos) & (wpos > qpos - WINDOW) & (wpos >= 0)
    pw = softmax(s, wmask).astype(MXU_DTYPE)
    o_win = (_dot_nt(pw[:, :wbuf], wst_ref[sq, kw:, :].astype(MXU_DTYPE))
             + _dot(pw[:, wbuf:], new_rows[:, kw:].astype(MXU_DTYPE)))

    gt = gt_ref[sq]
    o = gt[:, 0:1] * o_cmp + gt[:, 1:2] * o_slc + gt[:, 2:3] * o_win
    lane_g = lax.broadcasted_iota(jnp.int32, (rows_per_g, kw), 1) // HEAD_DIM
    acc = jnp.zeros((rows_per_g, kw), F32)
    for g in range(N_KV_HEADS):
        acc = acc + jnp.where(lane_g == g, o[g * rows_per_g:(g + 1) * rows_per_g], 0.0)
    o_ref[sq] = acc

    if wout_ref is not None:
        rolled = pltpu.roll(wst_ref[sq], wbuf - t_new, 1)
        wout_ref[sq, :, 0:wbuf - LANES] = rolled[:, 0:wbuf - LANES]
        new_t = pltpu.roll(new_rows.T, LANES - t_new, 1)
        ln = lax.broadcasted_iota(jnp.int32, new_t.shape, 1)
        wout_ref[sq, :, wbuf - LANES:wbuf] = jnp.where(ln >= LANES - t_new, new_t, rolled[:, wbuf - LANES:wbuf])


def _attn_sample(page_table, qn, qr, gates, gk, rsum, m_mat, e_mat, ab_pool, ab_new, cache_slc, slc_new,
                 win_state, win_new, *, n_sel, past_len, t_new, emit_window, seqs_per_step=2):
    nb, n_pages = page_table.shape
    nrow, kw = qn.shape[1], qn.shape[2]
    width = cache_slc.shape[1]
    page = cache_slc.shape[2]
    wbuf = win_state.shape[2]
    spb = seqs_per_step if nb % seqs_per_step == 0 else 1
    per_b = lambda shape: pl.BlockSpec((spb,) + shape, lambda i, pt: (i, 0, 0))

    def page_spec(shape, sq, p):
        return pl.BlockSpec(shape, lambda i, pt: (pt[(i * spb + sq) * n_pages + p], 0, 0))

    in_specs = [per_b((nrow, kw)), per_b((nrow, kw)), per_b((nrow, N_BRANCH)),
                pl.BlockSpec((1, LANES), lambda i, pt: (0, 0)),
                pl.BlockSpec(rsum.shape, lambda i, pt: (0, 0)),
                pl.BlockSpec(m_mat.shape, lambda i, pt: (0, 0)),
                pl.BlockSpec(e_mat.shape, lambda i, pt: (0, 0)),
                per_b(ab_new.shape[1:]), per_b(slc_new.shape[1:]),
                per_b((width, wbuf)), per_b(win_new.shape[1:])]
    in_specs += [page_spec((1,) + ab_pool.shape[1:], sq, p) for sq in range(spb) for p in range(n_pages)]
    in_specs += [page_spec((1, width, page), sq, p) for sq in range(spb) for p in range(n_pages)]
    out_specs = [per_b((nrow // N_KV_HEADS, kw))]
    out_shape = [jax.ShapeDtypeStruct((nb, nrow // N_KV_HEADS, kw), F32)]
    if emit_window:
        out_specs.append(per_b((width, wbuf)))
        out_shape.append(jax.ShapeDtypeStruct((nb, width, wbuf), F32))
    grid_spec = pltpu.PrefetchScalarGridSpec(
        num_scalar_prefetch=1, grid=(nb // spb,), in_specs=in_specs, out_specs=out_specs,
        scratch_shapes=[pltpu.VMEM((2 * spb, page, width), F32)])
    res = pl.pallas_call(
        functools.partial(_attn_sample_kernel, n_pages=n_pages, n_sel=n_sel, past_len=past_len, t_new=t_new,
                          emit_window=emit_window),
        grid_spec=grid_spec, out_shape=out_shape,
        compiler_params=_cparams(1), name="attn_sample",
    )(page_table.reshape(-1), qn, qr, gates, gk, rsum, m_mat, e_mat, ab_new, slc_new, win_state, win_new,
      *([ab_pool] * (spb * n_pages)), *([cache_slc] * (spb * n_pages)))
    return (res[0], res[1]) if emit_window else (res[0], None)


def _rope_tables(pos):
    half = ROPE_DIM // 2
    inv_freq = ROPE_THETA ** (-jnp.arange(half, dtype=F32) / half)
    ang = pos.astype(F32)[:, None] * inv_freq[None, :]
    cos, sin = jnp.cos(ang), jnp.sin(ang)
    n = pos.shape[0]
    z = jnp.zeros((n, HEAD_DIM - ROPE_DIM), F32)
    zh = jnp.zeros((n, half), F32)
    c_head = jnp.concatenate([cos, cos, z + 1.0], axis=1)
    dn_head = jnp.concatenate([zh, sin, z], axis=1)
    up_head = jnp.concatenate([-sin, zh, z], axis=1)
    tile2 = lambda a: jnp.concatenate([a, a], axis=1)
    return tile2(c_head), tile2(dn_head), tile2(up_head)


def _sel_matrices(n_cmp, n_cmp_pad, n_sel, n_keys):
    i = jnp.arange(n_cmp_pad)[:, None]
    j = jnp.arange(LANES)[None, :]
    m = ((i * CMP_STRIDE < j * SEL_BLOCK + SEL_BLOCK) & (i * CMP_STRIDE + CMP_LEN > j * SEL_BLOCK)
         & (i < n_cmp) & (j < n_sel))
    e = (jnp.arange(n_keys)[None, :] // SEL_BLOCK == jnp.arange(LANES)[:, None]) & (jnp.arange(LANES)[:, None] < n_sel)
    return m.astype(MXU_DTYPE), e.astype(MXU_DTYPE)


def _tile2(g):
    return jnp.concatenate([g, g]).reshape(1, LANES).astype(F32)


def kernel(x_prompt, x_sample, cache_kv_cmp, cache_kv_slc, state_kv_win, page_table, norm_mix, norm_ffn,
           ffn_w_gate, ffn_w_up, ffn_w_down, a_w_in, a_ln_g, a_ln_b, a_w_spatial, a_b_spatial, a_w_out,
           kv_norm, kv_w, k_norm_cmp, k_norm_slc, k_norm_win, cmp_pos_k, cmp_pos_v, cmp_w_k, cmp_w_v,
           b_w_in, b_q_norm, b_w_out):
    bf = MXU_DTYPE
    depth = norm_mix.shape[0]
    n_a = a_w_in.shape[0]
    bsz, t_p, d_model = x_prompt.shape
    nb, t_s, _ = x_sample.shape
    n_pages = page_table.shape[1]
    page = cache_kv_cmp.shape[1]
    past_len = n_pages * page
    width = 2 * N_KV_HEADS * HEAD_DIM
    kw = N_KV_HEADS * HEAD_DIM
    n_heads = b_w_out.shape[1] // HEAD_DIM
    rpg = n_heads // N_KV_HEADS
    dq = n_heads * HEAD_DIM
    assert N_KV_HEADS == 4 and rpg == 4 and t_p % CHUNK == 0 and CHUNK % t_s == 0 and t_s <= 8

    wg, wu, wd = ffn_w_gate.astype(bf), ffn_w_up.astype(bf), ffn_w_down.astype(bf)
    a_in, a_out = a_w_in.astype(bf), a_w_out.astype(bf)
    tril = jnp.tril(jnp.ones((CHUNK, CHUNK), bool))
    ws_p = jnp.where(tril[None, None], a_w_spatial, 0.0).astype(bf)
    bs_p = jnp.swapaxes(a_b_spatial, 1, 2)
    rep = CHUNK // t_s
    ws4 = jnp.where(tril[None, None, :t_s, :t_s], a_w_spatial[:, :, :t_s, :t_s], 0.0)
    ws_s = jnp.einsum('ab,lgts->lgatbs', jnp.eye(rep, dtype=F32), ws4).reshape(n_a, A_GROUPS, CHUNK, CHUNK).astype(bf)
    bs_s = jnp.tile(jnp.swapaxes(a_b_spatial[:, :, :t_s], 1, 2), (1, rep, 1))
    perm = jnp.array([(2 * p + side) * rpg + r for p in range(2) for r in range(rpg) for side in range(2)])
    wq = b_w_in[:, :, :dq].reshape(-1, d_model, n_heads, HEAD_DIM)[:, :, perm].reshape(-1, d_model, dq).astype(bf)
    wgate = jnp.pad(b_w_in[:, :, dq:], ((0, 0), (0, 0), (0, LANES - N_BRANCH * n_heads))).astype(bf)
    wo_pair = b_w_out.reshape(-1, n_heads, HEAD_DIM, d_model)[:, perm].reshape(-1, dq, d_model).astype(bf)
    wo_flat = b_w_out.astype(bf)
    kvw = kv_w.astype(bf)
    eye_g = jnp.eye(N_KV_HEADS, dtype=F32)
    w_bd = jnp.stack([jnp.einsum('ab,lde->ladbe', eye_g, w).reshape(CMP_LEN, kw, kw)
                      for w in (cmp_w_k, cmp_w_v)]).astype(bf)
    pos_tab = jnp.concatenate([jnp.tile(cmp_pos_k, (1, N_KV_HEADS)), jnp.tile(cmp_pos_v, (1, N_KV_HEADS))], axis=1)
    gk_cmp, gk_slc, gk_win = _tile2(k_norm_cmp), _tile2(k_norm_slc), _tile2(k_norm_win)

    def dense_a_layers(h, ws_eff, bs_eff, emit_v):
        vs = []
        for l in range(n_a):
            h, v = _a_layer(h, norm_mix[l], a_in[l], a_ln_g[l], a_ln_b[l], ws_eff[l], bs_eff[l], a_out[l],
                            norm_ffn[l], wg[l], wu[l], wd[l], emit_v=emit_v)
            vs.append(v)
        return h, vs

    tabs_p = _rope_tables(jnp.arange(t_p))
    h = x_prompt.reshape(bsz * t_p, d_model)
    h, _ = dense_a_layers(h, ws_p, bs_p, False)
    kv_cmp_p, kv_slc_p, kv_win_p, cmp_slabs, slc16, win16 = _kv_project(h, kv_norm, kvw, gk_slc, gk_win, tabs_p,
                                                                        seq_len=t_p)
    ab = _compress_partial(cmp_slabs, pos_tab, w_bd)
    n_sub = t_p // CMP_STRIDE
    kc, vc = _prompt_kc_vc(ab.reshape(bsz, n_sub, 2 * width), gk_cmp)
    n_sel_p = t_p // SEL_BLOCK
    m_p, e_p = _sel_matrices(n_sub - 1, n_sub, n_sel_p, t_p)
    for j in range(depth - n_a):
        l = n_a + j
        qn, qr, gates = _q_project(h, norm_mix[l], wq[j], wgate[j], _tile2(b_q_norm[j]) * ATTN_SCALE, tabs_p)
        o = _attn_prompt(qn, qr, gates, kc, vc, slc16.reshape(bsz, t_p, width), win16.reshape(bsz, t_p, width),
                         e_p.T, m_p.T, n_sel=n_sel_p)
        h = _ffn_layer(h, norm_ffn[l], wg[l], wu[l], wd[l], attn=o, w_o=wo_pair[j])
    y_prompt = h.reshape(bsz, t_p, d_model)
    row5 = lambda a, b_, t_: a.reshape(b_, t_, 2, N_KV_HEADS, HEAD_DIM)
    from_cm = lambda a: jnp.transpose(a.reshape(a.shape[0], 2, N_KV_HEADS, HEAD_DIM, a.shape[2]), (0, 4, 1, 2, 3))
    to_cm = lambda a: jnp.transpose(a, (0, 2, 3, 4, 1)).reshape(a.shape[0], width, a.shape[1])
    w_keep = min(WINDOW, t_p)
    kv_win_prompt = from_cm(kv_win_p[:, :, t_p - w_keep:])

    m_s = nb * t_s
    tabs_s = _rope_tables(past_len + (jnp.arange(nb * t_s) % t_s))
    hs = x_sample.reshape(m_s, d_model)
    hs, vs = dense_a_layers(hs, ws_s, bs_s, True)
    a_v_sample = jnp.stack(vs).reshape(n_a, nb, t_s, -1)
    kv_cmp_s, kv_slc_s, kv_win_s, cmp_slabs_s = _kv_project(hs, kv_norm, kvw, gk_slc, gk_win, tabs_s)
    n_pool = cache_kv_cmp.shape[0]
    sub_pp = page // CMP_STRIDE
    ab_pool = _compress_pages(to_cm(cache_kv_cmp), pos_tab, w_bd).reshape(n_pool, sub_pp, 2 * width)
    t_full = -(-(past_len + t_s) // SEL_BLOCK) * SEL_BLOCK
    n_new_sub = (t_full - past_len) // CMP_STRIDE
    new_cmp = jnp.pad(cmp_slabs_s.reshape(-1, nb, t_s, LANES), ((0, 0), (0, 0), (0, t_full - past_len - t_s), (0, 0)))
    ab_new = _compress_partial(new_cmp.reshape(-1, nb * (t_full - past_len), LANES), pos_tab, w_bd)
    ab_new = jnp.pad(ab_new.reshape(nb, n_new_sub, 2 * width), ((0, 0), (0, 8 - n_new_sub), (0, 0)))
    n_cmp_s = t_full // CMP_STRIDE - 1
    n_sel_s = t_full // SEL_BLOCK
    n_cmp_pad = 256
    n_keys = past_len + page
    m_sm, e_sm = _sel_matrices(n_cmp_s, n_cmp_pad, n_sel_s, n_keys)
    nrow = n_heads * t_s
    ridx = jnp.arange(nrow)
    rsum = ((ridx[:, None] // (rpg * t_s) == ridx[None, :] // (rpg * t_s))
            & (ridx[:, None] % t_s == ridx[None, :] % t_s)).astype(bf)
    pad8 = lambda a: jnp.pad(a.reshape(nb, t_s, width), ((0, 0), (0, 8 - t_s), (0, 0)))
    slc_new, win_new = pad8(kv_slc_s), pad8(kv_win_s)
    inv_perm = jnp.argsort(perm)

    def pad_q(q):
        q = q.reshape(nb, t_s, n_heads, HEAD_DIM)[:, :, inv_perm].reshape(nb, t_s, N_KV_HEADS, rpg, HEAD_DIM)
        q = jnp.transpose(q, (0, 2, 3, 1, 4))
        q = jnp.einsum('bgrtd,gk->bgrtkd', q, jnp.eye(N_KV_HEADS, dtype=q.dtype))
        return q.reshape(nb, nrow, kw)

    slc_cm, win_cm = to_cm(cache_kv_slc), to_cm(state_kv_win)
    kv_win_sample = None
    for j in range(depth - n_a):
        l = n_a + j
        qn, qr, gates = _q_project(hs, norm_mix[l], wq[j], wgate[j], _tile2(b_q_norm[j]) * ATTN_SCALE, tabs_s)
        g3 = gates[:N_BRANCH * n_heads].T.reshape(nb, t_s, N_KV_HEADS, rpg, N_BRANCH)
        g3 = jnp.transpose(g3, (0, 2, 3, 1, 4)).reshape(nb, nrow, N_BRANCH)
        o, w_out = _attn_sample(page_table, pad_q(qn), pad_q(qr), g3, gk_cmp, rsum, m_sm, e_sm,
                                ab_pool, ab_new, slc_cm, slc_new, win_cm, win_new,
                                n_sel=n_sel_s, past_len=past_len, t_new=t_s, emit_window=j == 0)
        if j == 0:
            kv_win_sample = from_cm(w_out)
        o = jnp.transpose(o.reshape(nb, rpg, t_s, N_KV_HEADS, HEAD_DIM), (0, 2, 3, 1, 4)).reshape(m_s, dq)
        hs = _ffn_layer(hs, norm_ffn[l], wg[l], wu[l], wd[l], attn=o.astype(bf), w_o=wo_flat[j])
    y_sample = hs.reshape(nb, t_s, d_model)

    return (y_prompt, y_sample, from_cm(kv_cmp_p), row5(kv_cmp_s, nb, t_s),
            from_cm(kv_slc_p), row5(kv_slc_s, nb, t_s), kv_win_prompt, kv_win_sample, a_v_sample)
```
